```python
import jax, jax.numpy as jnp
from jax import lax
import numpy as np

D_MODEL = 1024
BATCH = 4
SEQ = 8192
DEPTH = 2

CHUNK = 64
HEAD_DIM = 64
ROT_DIM = HEAD_DIM // 4
ROPE_THETA = 500000.0
N_HEADS_A = 6
N_HEADS_B = 5
N_HEADS_C = 5
WIDTH_A = N_HEADS_A * HEAD_DIM
WIDTH_B = N_HEADS_B * HEAD_DIM
WIDTH_C = N_HEADS_C * HEAD_DIM
MIX_WIDTH = WIDTH_A + WIDTH_B + WIDTH_C
IDX_HEADS = 8
IDX_DIM = 64
TOPK_MAX = 256
PREV_CHUNKS = 8
BAND = (PREV_CHUNKS + 1) * CHUNK
REL_CLIP = 128
Q_BLOCK = 128
N_BRANCH = 3
FFN_HIDDEN = -(-8 * D_MODEL // (3 * 256)) * 256
RMS_EPS = 1e-6
FORGET_BIAS_LO = 2.0
FORGET_BIAS_HI = 6.0
SPLIT_SIZES = (WIDTH_A, WIDTH_A, WIDTH_A, IDX_HEADS * IDX_DIM, IDX_DIM, IDX_HEADS,
               3 * WIDTH_B, 3 * WIDTH_C, N_HEADS_C, N_BRANCH * D_MODEL)
IN_WIDTH = sum(SPLIT_SIZES)

kernel_name = 'hybrid_gated_streaming_encoder'


def rmsnorm(x, g):
    xf = x.astype(jnp.float32)
    y = xf * lax.rsqrt(jnp.mean(xf * xf, axis=-1, keepdims=True) + RMS_EPS)
    return (y * g.astype(jnp.float32)).astype(x.dtype)


def rope_tables(positions):
    inv = ROPE_THETA ** (-jnp.arange(0, ROT_DIM, 2, dtype=jnp.float32) / ROT_DIM)
    ang = positions.astype(jnp.float32)[..., None] * inv
    return jnp.cos(ang), jnp.sin(ang)


def partial_rope(t, cos, sin):
    half = ROT_DIM // 2
    cs = cos[:, :, None, :].astype(t.dtype)
    sn = sin[:, :, None, :].astype(t.dtype)
    x1 = t[..., :half]
    x2 = t[..., half:ROT_DIM]
    return jnp.concatenate([x1 * cs - x2 * sn, x2 * cs + x1 * sn, t[..., ROT_DIM:]], axis=-1)


def dsa_attention(q, k, v, q_idx, k_idx, w_idx):
    B, S = q.shape[0], q.shape[1]
    topk = min(TOPK_MAX, S // 4)
    n_blk = S // Q_BLOCK
    key_chunk = jnp.arange(S) // CHUNK
    scale = HEAD_DIM ** -0.5
    kf = k_idx.astype(jnp.float32)
    gather = jax.vmap(lambda arr, ix: arr[ix])

    def block(i):
        start = i * Q_BLOCK
        qs = lax.dynamic_slice_in_dim(q, start, Q_BLOCK, axis=1)
        qis = lax.dynamic_slice_in_dim(q_idx, start, Q_BLOCK, axis=1).astype(jnp.float32)
        ws = lax.dynamic_slice_in_dim(w_idx, start, Q_BLOCK, axis=1).astype(jnp.float32)
        q_chunk = (start + jnp.arange(Q_BLOCK)) // CHUNK
        idx_scores = jnp.einsum('bqhs,bqh->bqs',
                                jax.nn.relu(jnp.einsum('bqhd,bsd->bqhs', qis, kf)), ws)
        adm = key_chunk[None, :] <= q_chunk[:, None]
        idx_scores = jnp.where(adm[None], idx_scores, -jnp.inf)
        _, sel = lax.top_k(idx_scores, topk)
        valid = (sel // CHUNK) <= q_chunk[None, :, None]
        k_sel = gather(k, sel)
        v_sel = gather(v, sel)
        logits = jnp.einsum('bqhd,bqkhd->bhqk', qs, k_sel).astype(jnp.float32) * scale
        logits = jnp.where(valid[:, None], logits, -jnp.inf)
        p = jax.nn.softmax(logits, axis=-1).astype(v.dtype)
        return jnp.einsum('bhqk,bqkhd->bqhd', p, v_sel)

    out = lax.map(block, jnp.arange(n_blk))
    return jnp.moveaxis(out, 0, 1).reshape(B, S, -1)


def chunk_band_attention(q, k, v, rel_bias):
    B, S, H, Dh = q.shape
    n_c = S // CHUNK
    pad = PREV_CHUNKS * CHUNK
    qc = q.reshape(B, n_c, CHUNK, H, Dh)
    kp = jnp.pad(k, ((0, 0), (pad, 0), (0, 0), (0, 0))).reshape(B, n_c + PREV_CHUNKS, CHUNK, H, Dh)
    vp = jnp.pad(v, ((0, 0), (pad, 0), (0, 0), (0, 0))).reshape(B, n_c + PREV_CHUNKS, CHUNK, H, Dh)
    kband = jnp.concatenate([kp[:, j:j + n_c] for j in range(PREV_CHUNKS + 1)], axis=2)
    vband = jnp.concatenate([vp[:, j:j + n_c] for j in range(PREV_CHUNKS + 1)], axis=2)
    logits = jnp.einsum('bcqhd,bckhd->bchqk', qc, kband).astype(jnp.float32) * (Dh ** -0.5)
    qi = jnp.arange(CHUNK)
    kj = jnp.arange(BAND)
    dist = jnp.clip(qi[:, None] + pad - kj[None, :], -REL_CLIP, REL_CLIP) + REL_CLIP
    bias = rel_bias[:, dist].astype(jnp.float32)
    band_chunk = jnp.arange(n_c)[:, None] - PREV_CHUNKS + kj[None, :] // CHUNK
    valid = band_chunk >= 0
    logits = jnp.where(valid[None, :, None, None, :], logits + bias[None, None], -jnp.inf)
    p = jax.nn.softmax(logits, axis=-1).astype(v.dtype)
    out = jnp.einsum('bchqk,bckhd->bcqhd', p, vband)
    return out.reshape(B, S, H * Dh)


def forgetting_attention(q, k, v, f_logit):
    B, S, H, Dh = q.shape
    n_blk = S // Q_BLOCK
    log_f = jax.nn.log_sigmoid(f_logit.astype(jnp.float32))
    F = jnp.moveaxis(jnp.cumsum(log_f, axis=1), 1, 2)
    key_pos = jnp.arange(S)
    scale = Dh ** -0.5

    def block(i):
        start = i * Q_BLOCK
        qs = lax.dynamic_slice_in_dim(q, start, Q_BLOCK, axis=1)
        Fq = lax.dynamic_slice_in_dim(F, start, Q_BLOCK, axis=2)
        logits = (jnp.einsum('bqhd,bshd->bhqs', qs, k).astype(jnp.float32) * scale
                  + Fq[..., :, None] - F[:, :, None, :])
        qpos = start + jnp.arange(Q_BLOCK)
        logits = jnp.where((key_pos[None, :] <= qpos[:, None])[None, None], logits, -jnp.inf)
        p = jax.nn.softmax(logits, axis=-1).astype(v.dtype)
        return jnp.einsum('bhqs,bshd->bqhd', p, v)

    out = lax.map(block, jnp.arange(n_blk))
    return jnp.moveaxis(out, 0, 1).reshape(B, S, H * Dh)


def hybrid_mixer(h, cos, sin, w_in, b_in, rel_bias, w_branch, w_out):
    B, S, D = h.shape
    proj = h @ w_in + b_in
    parts = []
    off = 0
    for size in SPLIT_SIZES:
        parts.append(proj[..., off:off + size])
        off += size
    qa, ka, va, qi, ki, wi, qkv_b, qkv_c, f_c, gate_logits = parts
    qa = partial_rope(qa.reshape(B, S, N_HEADS_A, HEAD_DIM), cos, sin)
    ka = partial_rope(ka.reshape(B, S, N_HEADS_A, HEAD_DIM), cos, sin)
    va = va.reshape(B, S, N_HEADS_A, HEAD_DIM)
    qi = partial_rope(qi.reshape(B, S, IDX_HEADS, IDX_DIM), cos, sin)
    ki = partial_rope(ki[:, :, None, :], cos, sin)[:, :, 0, :]
    wi = wi * (IDX_HEADS ** -0.5 * IDX_DIM ** -0.5)
    o_a = dsa_attention(qa, ka, va, qi, ki, wi)
    qkv_b = qkv_b.reshape(B, S, 3, N_HEADS_B, HEAD_DIM)
    o_b = chunk_band_attention(qkv_b[:, :, 0], qkv_b[:, :, 1], qkv_b[:, :, 2], rel_bias)
    qkv_c = qkv_c.reshape(B, S, 3, N_HEADS_C, HEAD_DIM)
    o_c = forgetting_attention(qkv_c[:, :, 0], qkv_c[:, :, 1], qkv_c[:, :, 2], f_c)
    gates = jax.nn.sigmoid(gate_logits.reshape(B, S, N_BRANCH, D))
    y = (gates[:, :, 0] * (o_a @ w_branch[:WIDTH_A])
         + gates[:, :, 1] * (o_b @ w_branch[WIDTH_A:WIDTH_A + WIDTH_B])
         + gates[:, :, 2] * (o_c @ w_branch[WIDTH_A + WIDTH_B:]))
    return y @ w_out


def swiglu(h, w_ffn_in, w_ffn_out):
    gu = h @ w_ffn_in
    g, u = gu[..., :FFN_HIDDEN], gu[..., FFN_HIDDEN:]
    return (jax.nn.silu(g) * u) @ w_ffn_out


def setup_inputs(seed: int = 0) -> dict:
    key = jax.random.key(seed)
    ks = jax.random.split(key, 16)
    f32 = jnp.float32
    D = D_MODEL
    nrm = jax.random.normal
    x = nrm(ks[0], (BATCH, SEQ, D), f32)
    c = nrm(ks[1], (BATCH, D), f32)
    offset = jax.random.randint(ks[2], (BATCH, 1), 0, 4096, dtype=jnp.int32)
    positions = offset + jnp.arange(SEQ, dtype=jnp.int32)[None, :]
    ada_w = 0.5 * D ** -0.5 * nrm(ks[3], (DEPTH, D, 6 * D), f32)
    ada_b = 0.01 * nrm(ks[4], (DEPTH, 6 * D), f32)
    norm1_g = 1.0 + 0.05 * nrm(ks[5], (DEPTH, D), f32)
    w_in = D ** -0.5 * nrm(ks[6], (DEPTH, D, IN_WIDTH), f32)
    f_off = sum(SPLIT_SIZES[:8])
    forget_b = jax.random.uniform(ks[7], (DEPTH, N_HEADS_C), f32, FORGET_BIAS_LO, FORGET_BIAS_HI)
    b_in = (0.01 * nrm(ks[8], (DEPTH, IN_WIDTH), f32)).at[:, f_off:f_off + N_HEADS_C].add(forget_b)
    rel_bias = 0.5 * nrm(ks[9], (DEPTH, N_HEADS_B, 2 * REL_CLIP + 1), f32)
    w_branch = WIDTH_B ** -0.5 * nrm(ks[10], (DEPTH, MIX_WIDTH, D), f32)
    w_out = D ** -0.5 * nrm(ks[11], (DEPTH, D, D), f32)
    norm2_g = 1.0 + 0.05 * nrm(ks[12], (DEPTH, D), f32)
    w_ffn_in = D ** -0.5 * nrm(ks[13], (DEPTH, D, 2 * FFN_HIDDEN), f32)
    w_ffn_out = FFN_HIDDEN ** -0.5 * nrm(ks[14], (DEPTH, FFN_HIDDEN, D), f32)
    final_g = 1.0 + 0.05 * nrm(ks[15], (D,), f32)
    return {'x': x, 'c': c, 'positions': positions, 'ada_w': ada_w, 'ada_b': ada_b,
            'norm1_g': norm1_g, 'w_in': w_in, 'b_in': b_in, 'rel_bias': rel_bias,
            'w_branch': w_branch, 'w_out': w_out, 'norm2_g': norm2_g,
            'w_ffn_in': w_ffn_in, 'w_ffn_out': w_ffn_out, 'final_g': final_g}


def reference(x, c, positions, ada_w, ada_b, norm1_g, w_in, b_in, rel_bias,
              w_branch, w_out, norm2_g, w_ffn_in, w_ffn_out, final_g):
    cos, sin = rope_tables(positions)
    cond = jax.nn.silu(c)
    for l in range(DEPTH):
        mod = (cond @ ada_w[l] + ada_b[l])[:, None, :]
        sh1, sc1, g1, sh2, sc2, g2 = jnp.split(mod, 6, axis=-1)
        h = rmsnorm(x, norm1_g[l]) * (1 + sc1) + sh1
        x = x + g1 * hybrid_mixer(h, cos, sin, w_in[l], b_in[l], rel_bias[l], w_branch[l], w_out[l])
        h = rmsnorm(x, norm2_g[l]) * (1 + sc2) + sh2
        x = x + g2 * swiglu(h, w_ffn_in[l], w_ffn_out[l])
    return rmsnorm(x, final_g)
```

```python
import functools

import numpy as np
import jax
import jax.numpy as jnp
from jax import lax
from jax.experimental import pallas as pl
from jax.experimental.pallas import tpu as pltpu

D_MODEL = 1024
CHUNK = 64
HEAD_DIM = 64
ROT_DIM = HEAD_DIM // 4
ROPE_THETA = 500000.0
N_HEADS_A = 6
N_HEADS_B = 5
N_HEADS_C = 5
WIDTH_A = N_HEADS_A * HEAD_DIM
WIDTH_B = N_HEADS_B * HEAD_DIM
WIDTH_C = N_HEADS_C * HEAD_DIM
IDX_HEADS = 8
IDX_DIM = 64
TOPK_MAX = 256
PREV_CHUNKS = 8
REL_CLIP = 128
N_BRANCH = 3
FFN_HIDDEN = -(-8 * D_MODEL // (3 * 256)) * 256
RMS_EPS = 1e-6
SPLIT_SIZES = (WIDTH_A, WIDTH_A, WIDTH_A, IDX_HEADS * IDX_DIM, IDX_DIM, IDX_HEADS,
               3 * WIDTH_B, 3 * WIDTH_C, N_HEADS_C, N_BRANCH * D_MODEL)

LANES = 128
PAIR = 2 * HEAD_DIM
HW = 384
BAND_PAD = PREV_CHUNKS * CHUNK
VMEM_LIMIT = 56 * 1024 * 1024

MXU_DTYPE = jnp.bfloat16
F32 = jnp.float32
NEG_BIG = -1e30
INT_MIN = -2 ** 31

OFF_QK = 0
OFF_QI = 2 * HW
OFF_KI = OFF_QI + IDX_HEADS * IDX_DIM
OFF_SM = OFF_KI + LANES
OFF_REST = OFF_SM + LANES
W_REST = 7 * HW
W_ALL = OFF_REST + W_REST
SM_W = 0
SM_F = 8


def _cparams(*sem):
    return pltpu.CompilerParams(dimension_semantics=sem, vmem_limit_bytes=VMEM_LIMIT)


def _dot(a, b):
    return jnp.dot(a, b, preferred_element_type=F32)


def _dot_nt(a, b):
    return lax.dot_general(a, b, (((1,), (1,)), ((), ())), preferred_element_type=F32)


def _sigmoid(x):
    return 1.0 / (1.0 + jnp.exp(-x))


def _rope_kernel(pos_ref, invf_ref, ma_ref, mb_ref, cos_ref, sa_ref, sb_ref):
    ang = pos_ref[...].astype(F32) * invf_ref[...]
    s = jnp.sin(ang)
    cos_ref[...] = jnp.cos(ang)
    sa_ref[...] = s * ma_ref[...]
    sb_ref[...] = s * mb_ref[...]


def _rope_tables(positions):
    t = positions.size
    tm = min(t, 1024)
    half = ROT_DIM // 2
    d = np.arange(LANES) % HEAD_DIM
    inv = (ROPE_THETA ** (-(2.0 * (d % half)) / ROT_DIM)).astype(np.float32)
    invf = np.where(d < ROT_DIM, inv, 0.0).astype(np.float32)[None, :]
    ma = np.where(d < half, -1.0, 0.0).astype(np.float32)[None, :]
    mb = np.where((d >= half) & (d < ROT_DIM), 1.0, 0.0).astype(np.float32)[None, :]
    row = pl.BlockSpec((1, LANES), lambda i: (0, 0))
    tab = pl.BlockSpec((tm, LANES), lambda i: (i, 0))
    return pl.pallas_call(
        _rope_kernel,
        grid=(t // tm,),
        in_specs=[pl.BlockSpec((tm, 1), lambda i: (i, 0)), row, row, row],
        out_specs=[tab, tab, tab],
        out_shape=[jax.ShapeDtypeStruct((t, LANES), F32)] * 3,
        compiler_params=_cparams("arbitrary"),
        name="rope_tables",
    )(positions.reshape(t, 1), jnp.asarray(invf), jnp.asarray(ma), jnp.asarray(mb))


def _ada_kernel(c_ref, w_ref, b_ref, o_ref):
    c = c_ref[...]
    cond = c * _sigmoid(c)
    o_ref[0] = jnp.dot(cond, w_ref[0], precision=lax.Precision.HIGHEST,
                       preferred_element_type=F32) + b_ref[0]


def _ada_mod(c, ada_w, ada_b):
    depth, d, n = ada_w.shape
    b = c.shape[0]
    rows = -(-b // 8) * 8
    cp = jnp.pad(c, ((0, rows - b), (0, 0)))
    tn = 1024
    out = pl.pallas_call(
        _ada_kernel,
        grid=(depth, n // tn),
        in_specs=[pl.BlockSpec((rows, d), lambda l, j: (0, 0)),
                  pl.BlockSpec((1, d, tn), lambda l, j: (l, 0, j)),
                  pl.BlockSpec((1, 1, tn), lambda l, j: (l, 0, j))],
        out_specs=pl.BlockSpec((1, rows, tn), lambda l, j: (l, 0, j)),
        out_shape=jax.ShapeDtypeStruct((depth, rows, n), F32),
        compiler_params=_cparams("arbitrary", "arbitrary"),
        name="ada_mod",
    )(cp, ada_w, ada_b.reshape(depth, 1, n))
    return out[:, :b]


def _norm_kernel(x_ref, g_ref, *rest, modulate):
    x = x_ref[0]
    y = x * lax.rsqrt(jnp.mean(x * x, axis=-1, keepdims=True) + RMS_EPS) * g_ref[...]
    if modulate:
        sc_ref, sh_ref, o_ref = rest
        y = y * (1.0 + sc_ref[0]) + sh_ref[0]
    else:
        (o_ref,) = rest
    o_ref[0] = y.astype(o_ref.dtype)


def _norm(x, g, mod, sc_blk, sh_blk, out_dtype):
    b, s, d = x.shape
    tm = min(s, 512)
    xspec = pl.BlockSpec((1, tm, d), lambda bi, i: (bi, i, 0))
    in_specs = [xspec, pl.BlockSpec((1, d), lambda bi, i: (0, 0))]
    args = [x, g.reshape(1, d)]
    if mod is not None:
        in_specs += [pl.BlockSpec((1, 1, d), lambda bi, i: (bi, 0, sc_blk)),
                     pl.BlockSpec((1, 1, d), lambda bi, i: (bi, 0, sh_blk))]
        args += [mod, mod]
    return pl.pallas_call(
        functools.partial(_norm_kernel, modulate=mod is not None),
        grid=(b, s // tm),
        in_specs=in_specs,
        out_specs=xspec,
        out_shape=jax.ShapeDtypeStruct((b, s, d), out_dtype),
        compiler_params=_cparams("arbitrary", "arbitrary"),
        name="norm_mod" if mod is not None else "norm_final",
    )(*args)


def _proj_kernel(h_ref, w_ref, b_ref, s_ref, cos_ref, sa_ref, sb_ref,
                 qk_ref, qi_ref, ki_ref, sm_ref, rest_ref):
    h = h_ref[...]

    def seg(a, n):
        return (_dot(h, w_ref[:, a:a + n]) + b_ref[:, a:a + n]) * s_ref[:, a:a + n]

    def rope_store(o_ref, o_off, a, n):
        r = seg(a, n)
        cos, sa, sb = cos_ref[...], sa_ref[...], sb_ref[...]
        for c in range(n // LANES):
            x = r[:, c * LANES:(c + 1) * LANES]
            y = x * cos + pltpu.roll(x, LANES - ROT_DIM // 2, 1) * sa + pltpu.roll(x, ROT_DIM // 2, 1) * sb
            o_ref[:, o_off + c * LANES:o_off + (c + 1) * LANES] = y.astype(o_ref.dtype)

    rope_store(qk_ref, 0, OFF_QK, HW)
    rope_store(qk_ref, HW, OFF_QK + HW, HW)
    rope_store(qi_ref, 0, OFF_QI, IDX_HEADS * IDX_DIM)
    rope_store(ki_ref, 0, OFF_KI, LANES)
    sm_ref[...] = seg(OFF_SM, LANES)
    for c in range(W_REST // HW):
        rest_ref[:, c * HW:(c + 1) * HW] = seg(OFF_REST + c * HW, HW).astype(rest_ref.dtype)


def _proj(h2d, w_all, b_all, s_all, cos, sa, sb):
    t, d = h2d.shape
    tm = min(t, 512)
    const = lambda shape: pl.BlockSpec(shape, lambda i: (0, 0))
    tok = lambda n: pl.BlockSpec((tm, n), lambda i: (i, 0))
    return pl.pallas_call(
        _proj_kernel,
        grid=(t // tm,),
        in_specs=[tok(d), const((d, W_ALL)), const((1, W_ALL)), const((1, W_ALL)),
                  tok(LANES), tok(LANES), tok(LANES)],
        out_specs=[tok(2 * HW), tok(IDX_HEADS * IDX_DIM), tok(LANES), tok(LANES), tok(W_REST)],
        out_shape=[jax.ShapeDtypeStruct((t, 2 * HW), MXU_DTYPE),
                   jax.ShapeDtypeStruct((t, IDX_HEADS * IDX_DIM), MXU_DTYPE),
                   jax.ShapeDtypeStruct((t, LANES), MXU_DTYPE),
                   jax.ShapeDtypeStruct((t, LANES), F32),
                   jax.ShapeDtypeStruct((t, W_REST), MXU_DTYPE)],
        compiler_params=_cparams("arbitrary"),
        name="in_proj",
    )(h2d, w_all, b_all, s_all, cos, sa, sb)


def _pack_in_weights(w_in, b_in):
    offs = np.cumsum((0,) + SPLIT_SIZES)
    part = lambda a, i: a[..., offs[i]:offs[i + 1]]
    padw = lambda a, n: jnp.pad(a, [(0, 0)] * (a.ndim - 1) + [(0, n - a.shape[-1])])

    def heads3(a, width):
        return [padw(a[..., j * width:(j + 1) * width], HW) for j in range(3)]

    def build(a):
        qa, ka, va, qi, ki, wi, qkv_b, qkv_c, f_c = (part(a, i) for i in range(9))
        small = padw(jnp.concatenate([wi, f_c], axis=-1), LANES)
        cols = [qa, ka, qi, ki, ki, small, va] + heads3(qkv_b, WIDTH_B) + heads3(qkv_c, WIDTH_C)
        return jnp.concatenate(cols, axis=-1)

    scale = np.ones((1, W_ALL), np.float32)
    qs = HEAD_DIM ** -0.5
    scale[:, OFF_QK:OFF_QK + HW] = qs
    scale[:, OFF_SM + SM_W:OFF_SM + SM_W + IDX_HEADS] = IDX_HEADS ** -0.5 * IDX_DIM ** -0.5
    scale[:, OFF_REST + HW:OFF_REST + 2 * HW] = qs
    scale[:, OFF_REST + 4 * HW:OFF_REST + 5 * HW] = qs
    w_all = build(w_in).astype(MXU_DTYPE)
    b_all = build(b_in[None, :])
    gates_w = part(w_in, 9).astype(MXU_DTYPE)
    gates_b = part(b_in[None, :], 9)
    return w_all, b_all, jnp.asarray(scale), gates_w, gates_b


def _cumsum_kernel(sm_ref, tri_ref, col_ref, row_ref, carry_ref):
    @pl.when(pl.program_id(1) == 0)
    def _():
        carry_ref[...] = jnp.zeros_like(carry_ref)

    x = sm_ref[0]
    lf = jnp.minimum(x, 0.0) - jnp.log(1.0 + jnp.exp(-jnp.abs(x)))
    cs = jnp.dot(tri_ref[...], lf, precision=lax.Precision.HIGHEST,
                 preferred_element_type=F32) + carry_ref[0:1, :]
    col_ref[0] = cs
    row_ref[0, 0] = cs.T[SM_F:SM_F + 8, :]
    carry_ref[...] = jnp.broadcast_to(cs[-1:, :], carry_ref.shape)


def _forget_cumsum(sm, tk):
    b, s, _ = sm.shape
    tri = jnp.asarray(np.tril(np.ones((tk, tk), np.float32)))
    return pl.pallas_call(
        _cumsum_kernel,
        grid=(b, s // tk),
        in_specs=[pl.BlockSpec((1, tk, LANES), lambda bi, i: (bi, i, 0)),
                  pl.BlockSpec((tk, tk), lambda bi, i: (0, 0))],
        out_specs=[pl.BlockSpec((1, tk, LANES), lambda bi, i: (bi, i, 0)),
                   pl.BlockSpec((1, 1, 8, tk), lambda bi, i: (bi, i, 0, 0))],
        out_shape=[jax.ShapeDtypeStruct((b, s, LANES), F32),
                   jax.ShapeDtypeStruct((b, s // tk, 8, tk), F32)],
        scratch_shapes=[pltpu.VMEM((8, LANES), F32)],
        compiler_params=_cparams("arbitrary", "arbitrary"),
        name="forget_cumsum",
    )(sm, tri)


def _even_lanes(rows):
    return lax.broadcasted_iota(jnp.int32, (rows, LANES), 1) < HEAD_DIM


def _store_masked_heads(dst_ref, src, n_heads):
    even = _even_lanes(src.shape[0])
    zero = jnp.zeros((src.shape[0], LANES), src.dtype)
    for h in range(n_heads):
        slab = src[:, (h // 2) * LANES:(h // 2 + 1) * LANES]
        dst_ref[h] = jnp.where(even, slab, zero) if h % 2 == 0 else jnp.where(even, zero, slab)


def _rep(col):
    return jnp.broadcast_to(col, (col.shape[0], LANES))


def _flash_step(qm_ref, m_ref, l_ref, acc_ref, k_blk, v_blk, bias_fn, n_heads):
    tq = acc_ref.shape[1]
    nslab = k_blk.shape[0] // LANES
    even = _even_lanes(tq)
    for p in range((n_heads + 1) // 2):
        kp = k_blk[:, p * LANES:(p + 1) * LANES]
        vp = v_blk[:, p * LANES:(p + 1) * LANES]
        pvs, alphas = [], []
        for h in range(2 * p, min(2 * p + 2, n_heads)):
            s = bias_fn(h, _dot_nt(qm_ref[h], kp))
            m_old = m_ref[h]
            m_new = jnp.maximum(m_old, _rep(jnp.max(s, axis=1, keepdims=True)))
            alpha = jnp.exp(m_old - m_new)
            pe = jnp.concatenate(
                [jnp.exp(s[:, c * LANES:(c + 1) * LANES] - m_new) for c in range(nslab)], axis=1)
            l_ref[h] = alpha * l_ref[h] + _rep(jnp.sum(pe, axis=1, keepdims=True))
            m_ref[h] = m_new
            pvs.append(_dot(pe.astype(vp.dtype), vp))
            alphas.append(alpha)
        if len(pvs) == 2:
            acc_ref[p] = acc_ref[p] * jnp.where(even, alphas[0], alphas[1]) + jnp.where(even, pvs[0], pvs[1])
        else:
            acc_ref[p] = acc_ref[p] * alphas[0] + pvs[0]


def _flash_init(m_ref, l_ref, acc_ref):
    m_ref[...] = jnp.full(m_ref.shape, NEG_BIG, F32)
    l_ref[...] = jnp.zeros(l_ref.shape, F32)
    acc_ref[...] = jnp.zeros(acc_ref.shape, F32)


def _flash_finish(o_ref, l_ref, acc_ref, n_heads):
    even = _even_lanes(acc_ref.shape[1])
    for p in range(acc_ref.shape[0]):
        l = jnp.where(even, l_ref[2 * p], l_ref[2 * p + 1]) if 2 * p + 1 < n_heads else l_ref[2 * p]
        o_ref[0, :, p * LANES:(p + 1) * LANES] = (acc_ref[p] / l).astype(o_ref.dtype)


def _dsa_kernel(qi_ref, ki_ref, sm_ref, q_ref, k_ref, v_ref, tri_ref, o_ref,
                keys_ref, qim_ref, wb_ref, qm_ref, m_ref, l_ref, acc_ref, *, tq, tk, topk):
    i = pl.program_id(1)
    nslab = tk // LANES
    nkb = (i * tq + tq + tk - 1) // tk
    lane = lax.broadcasted_iota(jnp.int32, (tq, LANES), 1)
    row = lax.broadcasted_iota(jnp.int32, (tq, LANES), 0)
    q_chunk = (i * tq + row) // CHUNK

    _store_masked_heads(qim_ref, qi_ref[0], IDX_HEADS)
    _store_masked_heads(qm_ref, q_ref[0], N_HEADS_A)
    sm = sm_ref[0]
    for h in range(IDX_HEADS):
        wb_ref[h] = _rep(sm[:, SM_W + h:SM_W + h + 1])

    def score_body(kb, carry):
        k_blk = ki_ref[0, pl.ds(pl.multiple_of(kb * tk, tk), tk), :]
        accs = [jnp.zeros((tq, LANES), F32) for _ in range(nslab)]
        for h in range(IDX_HEADS):
            s = _dot_nt(qim_ref[h], k_blk)
            wbh = wb_ref[h]
            for c in range(nslab):
                accs[c] = accs[c] + wbh * jnp.maximum(s[:, c * LANES:(c + 1) * LANES], 0.0)
        for c in range(nslab):
            k_chunk = (kb * tk + c * LANES + lane) // CHUNK
            sc = jnp.where(k_chunk <= q_chunk, accs[c], -jnp.inf)
            bits = lax.bitcast_convert_type(sc, jnp.int32)
            keys_ref[kb * nslab + c] = bits ^ ((bits >> 31) & jnp.int32(0x7FFFFFFF))
        return carry

    lax.fori_loop(0, nkb, score_body, 0)

    kcap = jnp.minimum(topk, (q_chunk + 1) * CHUNK).astype(F32)

    def count(pred):
        def body(sidx, acc):
            return acc + jnp.where(pred(keys_ref[sidx]), 1.0, 0.0)
        acc = lax.fori_loop(0, nkb * nslab, body, jnp.zeros((tq, LANES), F32))
        return _rep(jnp.sum(acc, axis=1, keepdims=True))

    def bit_body(t, prefix):
        cand_u = prefix | (jnp.int32(1) << (31 - t))
        cand_s = cand_u ^ jnp.int32(INT_MIN)
        cnt = count(lambda blk: blk >= cand_s)
        return jnp.where(cnt >= kcap, cand_u, prefix)

    prefix = lax.fori_loop(0, 32, bit_body, jnp.zeros((tq, LANES), jnp.int32))
    thr = prefix ^ jnp.int32(INT_MIN)
    need = kcap - count(lambda blk: blk > thr)

    _flash_init(m_ref, l_ref, acc_ref)
    ones = jnp.ones((tk, LANES), MXU_DTYPE)

    def att_body(kb, tie_seen):
        blks = [keys_ref[kb * nslab + c] for c in range(nslab)]
        eq = jnp.concatenate([jnp.where(b == thr, 1.0, 0.0) for b in blks], axis=1).astype(MXU_DTYPE)
        rank = _dot(eq, tri_ref[...])
        bias = jnp.concatenate(
            [jnp.where((blks[c] > thr) | ((blks[c] == thr)
                                          & (rank[:, c * LANES:(c + 1) * LANES] + tie_seen <= need)),
                       0.0, -jnp.inf) for c in range(nslab)], axis=1)
        start = pl.multiple_of(kb * tk, tk)
        _flash_step(qm_ref, m_ref, l_ref, acc_ref,
                    k_ref[0, pl.ds(start, tk), :], v_ref[0, pl.ds(start, tk), :],
                    lambda h, s: s + bias, N_HEADS_A)
        return tie_seen + _dot(eq, ones)

    lax.fori_loop(0, nkb, att_body, jnp.zeros((tq, LANES), F32))
    _flash_finish(o_ref, l_ref, acc_ref, N_HEADS_A)


def _dsa_attention(qk, qi, ki, sm, rest, topk):
    b, s, _ = qk.shape
    tq, tk = 128, 256
    tq, tk = min(tq, s), min(tk, s)
    tri = jnp.asarray(np.triu(np.ones((tk, tk), np.float32))).astype(MXU_DTYPE)
    qblk = lambda n, c: pl.BlockSpec((1, tq, n), lambda bi, i: (bi, i, c))
    full = lambda n, c: pl.BlockSpec((1, s, n), lambda bi, i: (bi, 0, c))
    return pl.pallas_call(
        functools.partial(_dsa_kernel, tq=tq, tk=tk, topk=topk),
        grid=(b, s // tq),
        in_specs=[qblk(IDX_HEADS * IDX_DIM, 0), full(LANES, 0), qblk(LANES, 0),
                  qblk(HW, 0), full(HW, 1), full(HW, 0),
                  pl.BlockSpec((tk, tk), lambda bi, i: (0, 0))],
        out_specs=qblk(HW, 0),
        out_shape=jax.ShapeDtypeStruct((b, s, HW), MXU_DTYPE),
        scratch_shapes=[pltpu.VMEM((s // LANES, tq, LANES), jnp.int32),
                        pltpu.VMEM((IDX_HEADS, tq, LANES), MXU_DTYPE),
                        pltpu.VMEM((IDX_HEADS, tq, LANES), F32),
                        pltpu.VMEM((N_HEADS_A, tq, LANES), MXU_DTYPE),
                        pltpu.VMEM((N_HEADS_A, tq, LANES), F32),
                        pltpu.VMEM((N_HEADS_A, tq, LANES), F32),
                        pltpu.VMEM((HW // LANES, tq, LANES), F32)],
        compiler_params=_cparams("arbitrary", "arbitrary"),
        name="dsa_attention",
    )(qi, ki, sm, qk, qk, rest, tri)


def _band_kernel(q_ref, k_ref, v_ref, bt_ref, o_ref, qm_ref, *, tq):
    i = pl.program_id(1)
    win = tq + BAND_PAD
    start = pl.multiple_of(i * tq, tq)
    _store_masked_heads(qm_ref, q_ref[0], N_HEADS_B)
    k_win = k_ref[0, pl.ds(start, win), :]
    v_win = v_ref[0, pl.ds(start, win), :]
    col = lax.broadcasted_iota(jnp.int32, (tq, win), 1)
    in_seq = col >= BAND_PAD - i * tq
    even = _even_lanes(tq)
    for p in range(HW // LANES):
        kp = k_win[:, p * LANES:(p + 1) * LANES]
        vp = v_win[:, p * LANES:(p + 1) * LANES]
        outs = []
        for h in range(2 * p, min(2 * p + 2, N_HEADS_B)):
            s = jnp.where(in_seq, _dot_nt(qm_ref[h], kp) + bt_ref[h], NEG_BIG)
            pe = jnp.exp(s - jnp.max(s, axis=1, keepdims=True))
            outs.append(_dot(pe.astype(vp.dtype), vp) / jnp.sum(pe, axis=1, keepdims=True))
        o = jnp.where(even, outs[0], outs[1]) if len(outs) == 2 else outs[0]
        o_ref[0, :, p * LANES:(p + 1) * LANES] = o.astype(o_ref.dtype)


def _band_bias_table(rel_bias, tq):
    win = tq + BAND_PAD
    r = np.arange(tq)[:, None]
    c = np.arange(win)[None, :]
    dist = np.clip(r + BAND_PAD - c, -REL_CLIP, REL_CLIP) + REL_CLIP
    back = r // CHUNK + PREV_CHUNKS - c // CHUNK
    in_band = (back >= 0) & (back <= PREV_CHUNKS)
    return jnp.where(jnp.asarray(in_band)[None], rel_bias[:, dist], NEG_BIG)


def _band_attention(rest, rel_bias):
    b, s, _ = rest.shape
    tq = min(128, s)
    win = tq + BAND_PAD
    kv = jnp.pad(rest[:, :, 2 * HW:4 * HW], ((0, 0), (BAND_PAD, 0), (0, 0)))
    bt = _band_bias_table(rel_bias, tq)
    return pl.pallas_call(
        functools.partial(_band_kernel, tq=tq),
        grid=(b, s // tq),
        in_specs=[pl.BlockSpec((1, tq, HW), lambda bi, i: (bi, i, 1)),
                  pl.BlockSpec((1, s + BAND_PAD, HW), lambda bi, i: (bi, 0, 0)),
                  pl.BlockSpec((1, s + BAND_PAD, HW), lambda bi, i: (bi, 0, 1)),
                  pl.BlockSpec((N_HEADS_B, tq, win), lambda bi, i: (0, 0, 0))],
        out_specs=pl.BlockSpec((1, tq, HW), lambda bi, i: (bi, i, 0)),
        out_shape=jax.ShapeDtypeStruct((b, s, HW), MXU_DTYPE),
        scratch_shapes=[pltpu.VMEM((N_HEADS_B, tq, LANES), MXU_DTYPE)],
        compiler_params=_cparams("arbitrary", "arbitrary"),
        name="band_attention",
    )(rest, kv, kv, bt)


def _forget_kernel(q_ref, k_ref, v_ref, fcol_ref, frow_ref, o_ref,
                   qm_ref, fq_ref, m_ref, l_ref, acc_ref, *, tq):
    i = pl.program_id(1)
    _store_masked_heads(qm_ref, q_ref[0], N_HEADS_C)
    fcol = fcol_ref[0]
    for h in range(N_HEADS_C):
        fq_ref[h] = _rep(fcol[:, SM_F + h:SM_F + h + 1])
    _flash_init(m_ref, l_ref, acc_ref)
    nslab = tq // LANES

    def step(kb, causal):
        start = pl.multiple_of(kb * tq, tq)
        f_k = frow_ref[0, kb]
        if causal:
            tri = (lax.broadcasted_iota(jnp.int32, (tq, tq), 1)
                   <= lax.broadcasted_iota(jnp.int32, (tq, tq), 0))

        def bias_fn(h, s):
            fq = fq_ref[h]
            out = jnp.concatenate(
                [s[:, c * LANES:(c + 1) * LANES] + (fq - f_k[h:h + 1, c * LANES:(c + 1) * LANES])
                 for c in range(nslab)], axis=1)
            return jnp.where(tri, out, -jnp.inf) if causal else out

        _flash_step(qm_ref, m_ref, l_ref, acc_ref,
                    k_ref[0, pl.ds(start, tq), :], v_ref[0, pl.ds(start, tq), :],
                    bias_fn, N_HEADS_C)

    def body(kb, carry):
        step(kb, False)
        return carry

    lax.fori_loop(0, i, body, 0)
    step(i, True)
    _flash_finish(o_ref, l_ref, acc_ref, N_HEADS_C)


def _forget_attention(rest, fcol, frow, tq):
    b, s, _ = rest.shape
    return pl.pallas_call(
        functools.partial(_forget_kernel, tq=tq),
        grid=(b, s // tq),
        in_specs=[pl.BlockSpec((1, tq, HW), lambda bi, i: (bi, i, 4)),
                  pl.BlockSpec((1, s, HW), lambda bi, i: (bi, 0, 5)),
                  pl.BlockSpec((1, s, HW), lambda bi, i: (bi, 0, 6)),
                  pl.BlockSpec((1, tq, LANES), lambda bi, i: (bi, i, 0)),
                  pl.BlockSpec((1, s // tq, 8, tq), lambda bi, i: (bi, 0, 0, 0))],
        out_specs=pl.BlockSpec((1, tq, HW), lambda bi, i: (bi, i, 0)),
        out_shape=jax.ShapeDtypeStruct((b, s, HW), MXU_DTYPE),
        scratch_shapes=[pltpu.VMEM((N_HEADS_C, tq, LANES), MXU_DTYPE),
                        pltpu.VMEM((N_HEADS_C, tq, LANES), F32),
                        pltpu.VMEM((N_HEADS_C, tq, LANES), F32),
                        pltpu.VMEM((N_HEADS_C, tq, LANES), F32),
                        pltpu.VMEM((HW // LANES, tq, LANES), F32)],
        compiler_params=_cparams("arbitrary", "arbitrary"),
        name="forget_attention",
    )(rest, rest, rest, fcol, frow)


def _merge_kernel(x_ref, h_ref, oa_ref, ob_ref, oc_ref, wg_ref, bg_ref, wb_ref, wo_ref, g_ref, o_ref):
    h = h_ref[0]
    d = h.shape[1]
    y = None
    for br, o_br in enumerate((oa_ref, ob_ref, oc_ref)):
        gate = _sigmoid(_dot(h, wg_ref[:, br * d:(br + 1) * d]) + bg_ref[:, br * d:(br + 1) * d])
        term = gate * _dot(o_br[0], wb_ref[br])
        y = term if y is None else y + term
    o_ref[0] = x_ref[0] + g_ref[0] * _dot(y.astype(wo_ref.dtype), wo_ref[...])


def _merge(x, h, o_a, o_b, o_c, gates_w, gates_b, wb, w_out, mod, g_blk):
    b, s, d = x.shape
    tm = min(s, 512)
    tok = lambda n: pl.BlockSpec((1, tm, n), lambda bi, i: (bi, i, 0))
    const2 = lambda shape: pl.BlockSpec(shape, lambda bi, i: (0, 0))
    return pl.pallas_call(
        _merge_kernel,
        grid=(b, s // tm),
        in_specs=[tok(d), tok(d), tok(HW), tok(HW), tok(HW),
                  const2((d, N_BRANCH * d)), const2((1, N_BRANCH * d)),
                  pl.BlockSpec((N_BRANCH, HW, d), lambda bi, i: (0, 0, 0)),
                  const2((d, d)),
                  pl.BlockSpec((1, 1, d), lambda bi, i: (bi, 0, g_blk))],
        out_specs=tok(d),
        out_shape=jax.ShapeDtypeStruct((b, s, d), F32),
        compiler_params=_cparams("arbitrary", "arbitrary"),
        name="gated_merge",
    )(x, h, o_a, o_b, o_c, gates_w, gates_b, wb, w_out, mod)


def _ffn_kernel(x_ref, h_ref, wg_ref, wu_ref, wo_ref, g_ref, o_ref, acc_ref):
    j = pl.program_id(2)

    @pl.when(j == 0)
    def _():
        acc_ref[...] = jnp.zeros_like(acc_ref)

    h = h_ref[0]
    g = _dot(h, wg_ref[...])
    u = _dot(h, wu_ref[...])
    act = (g * _sigmoid(g)) * u
    acc_ref[...] += _dot(act.astype(wo_ref.dtype), wo_ref[...])

    @pl.when(j == pl.num_programs(2) - 1)
    def _():
        o_ref[0] = x_ref[0] + g_ref[0] * acc_ref[...]


def _ffn(x, h, w_in, w_out, mod, g_blk):
    b, s, d = x.shape
    tm = min(s, 1024)
    tf = 256
    nf = FFN_HIDDEN // tf
    tok = lambda: pl.BlockSpec((1, tm, d), lambda bi, i, j: (bi, i, 0))
    return pl.pallas_call(
        _ffn_kernel,
        grid=(b, s // tm, nf),
        in_specs=[tok(), tok(),
                  pl.BlockSpec((d, tf), lambda bi, i, j: (0, j)),
                  pl.BlockSpec((d, tf), lambda bi, i, j: (0, nf + j)),
                  pl.BlockSpec((tf, d), lambda bi, i, j: (j, 0)),
                  pl.BlockSpec((1, 1, d), lambda bi, i, j: (bi, 0, g_blk))],
        out_specs=tok(),
        out_shape=jax.ShapeDtypeStruct((b, s, d), F32),
        scratch_shapes=[pltpu.VMEM((tm, d), F32)],
        compiler_params=_cparams("arbitrary", "arbitrary", "arbitrary"),
        name="swiglu_ffn",
    )(x, h, w_in, w_in, w_out, mod)


def _pack_branch_weights(w_branch):
    pad = lambda a: jnp.pad(a, ((0, HW - a.shape[0]), (0, 0)))
    return jnp.stack([pad(w_branch[:WIDTH_A]),
                      pad(w_branch[WIDTH_A:WIDTH_A + WIDTH_B]),
                      pad(w_branch[WIDTH_A + WIDTH_B:])]).astype(MXU_DTYPE)


def kernel(x, c, positions, ada_w, ada_b, norm1_g, w_in, b_in, rel_bias, w_branch, w_out,
           norm2_g, w_ffn_in, w_ffn_out, final_g):
    b, s, d = x.shape
    depth = ada_w.shape[0]
    topk = min(TOPK_MAX, s // 4)
    t_forget = min(256, s)
    cos, sa, sb = _rope_tables(positions)
    mod_all = _ada_mod(c, ada_w, ada_b)
    for l in range(depth):
        mod = mod_all[l].reshape(b, 1, 6 * d)
        w_all, b_all, s_all, gates_w, gates_b = _pack_in_weights(w_in[l], b_in[l])
        h = _norm(x, norm1_g[l], mod, 1, 0, MXU_DTYPE)
        qk, qi, ki, sm, rest = _proj(h.reshape(b * s, d), w_all, b_all, s_all, cos, sa, sb)
        qk, qi, ki, sm, rest = (a.reshape(b, s, a.shape[-1]) for a in (qk, qi, ki, sm, rest))
        o_a = _dsa_attention(qk, qi, ki, sm, rest, topk)
        o_b = _band_attention(rest, rel_bias[l])
        fcol, frow = _forget_cumsum(sm, t_forget)
        o_c = _forget_attention(rest, fcol, frow, t_forget)
        x = _merge(x, h, o_a, o_b, o_c, gates_w, gates_b, _pack_branch_weights(w_branch[l]),
                   w_out[l].astype(MXU_DTYPE), mod, 2)
        h = _norm(x, norm2_g[l], mod, 4, 3, MXU_DTYPE)
        x = _ffn(x, h, w_ffn_in[l].astype(MXU_DTYPE), w_ffn_out[l].astype(MXU_DTYPE), mod, 5)
    return _norm(x, final_g, None, 0, 0, x.dtype)
```

```python
import functools

import numpy as np
import jax
import jax.numpy as jnp
from jax import lax
from jax.experimental import pallas as pl
from jax.experimental.pallas import tpu as pltpu

D_MODEL = 1024
CHUNK = 64
HEAD_DIM = 64
ROT_DIM = HEAD_DIM // 4
ROPE_THETA = 500000.0
N_HEADS_A = 6
N_HEADS_B = 5
N_HEADS_C = 5
WIDTH_A = N_HEADS_A * HEAD_DIM
WIDTH_B = N_HEADS_B * HEAD_DIM
WIDTH_C = N_HEADS_C * HEAD_DIM
IDX_HEADS = 8
IDX_DIM = 64
TOPK_MAX = 256
PREV_CHUNKS = 8
REL_CLIP = 128
N_BRANCH = 3
FFN_HIDDEN = -(-8 * D_MODEL // (3 * 256)) * 256
RMS_EPS = 1e-6
SPLIT_SIZES = (WIDTH_A, WIDTH_A, WIDTH_A, IDX_HEADS * IDX_DIM, IDX_DIM, IDX_HEADS,
               3 * WIDTH_B, 3 * WIDTH_C, N_HEADS_C, N_BRANCH * D_MODEL)

LANES = 128
SUBLANES = 8
HW = 384
BAND_PAD = PREV_CHUNKS * CHUNK
VMEM_LIMIT = 56 * 1024 * 1024

MXU_DTYPE = jnp.bfloat16
F32 = jnp.float32
NEG_BIG = -1e30
INT_MIN = -2 ** 31

OFF_QK = 0
OFF_QI = 2 * HW
OFF_KI = OFF_QI + IDX_HEADS * IDX_DIM
OFF_SM = OFF_KI + LANES
OFF_VA = OFF_SM + LANES
OFF_REST = OFF_VA + HW
W_REST = 6 * HW
W_ALL = OFF_REST + W_REST
SM_W = 0
SM_F = 8
DSA_BLOCK = 256
FORGET_BLOCK = 256
COUNT_ROWS = 4 * SUBLANES


def _cparams(*sem):
    return pltpu.CompilerParams(dimension_semantics=sem, vmem_limit_bytes=VMEM_LIMIT)


def _dot(a, b):
    return jnp.dot(a, b, preferred_element_type=F32)


def _dot_nt(a, b):
    return lax.dot_general(a, b, (((1,), (1,)), ((), ())), preferred_element_type=F32)


def _sigmoid(x):
    return 1.0 / (1.0 + jnp.exp(-x))


def _rope_kernel(pos_ref, invf_ref, ma_ref, mb_ref, cos_ref, sa_ref, sb_ref):
    ang = pos_ref[...].astype(F32) * invf_ref[...]
    s = jnp.sin(ang)
    cos_ref[...] = jnp.cos(ang)
    sa_ref[...] = s * ma_ref[...]
    sb_ref[...] = s * mb_ref[...]


def _rope_tables(positions):
    t = positions.size
    tm = min(t, 1024)
    half = ROT_DIM // 2
    d = np.arange(LANES) % HEAD_DIM
    inv = (ROPE_THETA ** (-(2.0 * (d % half)) / ROT_DIM)).astype(np.float32)
    invf = np.where(d < ROT_DIM, inv, 0.0).astype(np.float32)[None, :]
    ma = np.where(d < half, -1.0, 0.0).astype(np.float32)[None, :]
    mb = np.where((d >= half) & (d < ROT_DIM), 1.0, 0.0).astype(np.float32)[None, :]
    row = pl.BlockSpec((1, LANES), lambda i: (0, 0))
    tab = pl.BlockSpec((tm, LANES), lambda i: (i, 0))
    return pl.pallas_call(
        _rope_kernel,
        grid=(t // tm,),
        in_specs=[pl.BlockSpec((tm, 1), lambda i: (i, 0)), row, row, row],
        out_specs=[tab, tab, tab],
        out_shape=[jax.ShapeDtypeStruct((t, LANES), F32)] * 3,
        compiler_params=_cparams("arbitrary"),
        name="rope_tables",
    )(positions.reshape(t, 1), jnp.asarray(invf), jnp.asarray(ma), jnp.asarray(mb))


def _ada_kernel(c_ref, w_ref, b_ref, o_ref):
    c = c_ref[...]
    cond = c * _sigmoid(c)
    o_ref[0] = jnp.dot(cond, w_ref[0], precision=lax.Precision.HIGHEST,
                       preferred_element_type=F32) + b_ref[0]


def _ada_mod(c, ada_w, ada_b):
    depth, d, n = ada_w.shape
    b = c.shape[0]
    rows = -(-b // SUBLANES) * SUBLANES
    cp = jnp.pad(c, ((0, rows - b), (0, 0)))
    tn = 1024
    out = pl.pallas_call(
        _ada_kernel,
        grid=(depth, n // tn),
        in_specs=[pl.BlockSpec((rows, d), lambda l, j: (0, 0)),
                  pl.BlockSpec((1, d, tn), lambda l, j: (l, 0, j)),
                  pl.BlockSpec((1, 1, tn), lambda l, j: (l, 0, j))],
        out_specs=pl.BlockSpec((1, rows, tn), lambda l, j: (l, 0, j)),
        out_shape=jax.ShapeDtypeStruct((depth, rows, n), F32),
        compiler_params=_cparams("arbitrary", "arbitrary"),
        name="ada_mod",
    )(cp, ada_w, ada_b.reshape(depth, 1, n))
    return out[:, :b]


def _norm_kernel(x_ref, g_ref, *rest, modulate):
    x = x_ref[0]
    y = x * lax.rsqrt(jnp.mean(x * x, axis=-1, keepdims=True) + RMS_EPS) * g_ref[...]
    if modulate:
        sc_ref, sh_ref, o_ref = rest
        y = y * (1.0 + sc_ref[0]) + sh_ref[0]
    else:
        (o_ref,) = rest
    o_ref[0] = y.astype(o_ref.dtype)


def _norm(x, g, mod, sc_blk, sh_blk, out_dtype):
    b, s, d = x.shape
    tm = min(s, 512)
    xspec = pl.BlockSpec((1, tm, d), lambda bi, i: (bi, i, 0))
    in_specs = [xspec, pl.BlockSpec((1, d), lambda bi, i: (0, 0))]
    args = [x, g.reshape(1, d)]
    if mod is not None:
        in_specs += [pl.BlockSpec((1, 1, d), lambda bi, i: (bi, 0, sc_blk)),
                     pl.BlockSpec((1, 1, d), lambda bi, i: (bi, 0, sh_blk))]
        args += [mod, mod]
    return pl.pallas_call(
        functools.partial(_norm_kernel, modulate=mod is not None),
        grid=(b, s // tm),
        in_specs=in_specs,
        out_specs=xspec,
        out_shape=jax.ShapeDtypeStruct((b, s, d), out_dtype),
        compiler_params=_cparams("arbitrary", "arbitrary"),
        name="norm_mod" if mod is not None else "norm_final",
    )(*args)


def _proj_kernel(h_ref, w_ref, b_ref, s_ref, cos_ref, sa_ref, sb_ref,
                 qt_ref, k_ref, vt_ref, qit_ref, ki_ref, wt_ref, sm_ref, rest_ref, *, tk):
    h = h_ref[...]
    tm = h.shape[0]

    def seg(a, n):
        return (_dot(h, w_ref[:, a:a + n]) + b_ref[:, a:a + n]) * s_ref[:, a:a + n]

    def rope(a, n):
        r = seg(a, n)
        cos, sa, sb = cos_ref[...], sa_ref[...], sb_ref[...]
        out = []
        for c in range(n // LANES):
            x = r[:, c * LANES:(c + 1) * LANES]
            out.append(x * cos + pltpu.roll(x, LANES - ROT_DIM // 2, 1) * sa
                       + pltpu.roll(x, ROT_DIM // 2, 1) * sb)
        return jnp.concatenate(out, axis=1)

    qt_ref[...] = rope(OFF_QK, HW).T.astype(qt_ref.dtype)
    k_ref[...] = rope(OFF_QK + HW, HW).astype(k_ref.dtype)
    qit_ref[...] = rope(OFF_QI, IDX_HEADS * IDX_DIM).T.astype(qit_ref.dtype)
    ki_ref[...] = rope(OFF_KI, LANES)[:, :IDX_DIM].astype(ki_ref.dtype)
    sm = seg(OFF_SM, LANES)
    sm_ref[...] = sm
    wt_ref[...] = sm.T[SM_W:SM_W + IDX_HEADS, :]
    v = seg(OFF_VA, HW)
    for j in range(tm // tk):
        vt_ref[j] = v[j * tk:(j + 1) * tk, :].T.astype(vt_ref.dtype)
    for c in range(W_REST // HW):
        rest_ref[:, c * HW:(c + 1) * HW] = seg(OFF_REST + c * HW, HW).astype(rest_ref.dtype)


def _proj(h2d, w_all, b_all, s_all, cos, sa, sb, tk):
    t, d = h2d.shape
    tm = min(t, 512)
    nqi = IDX_HEADS * IDX_DIM
    const = lambda shape: pl.BlockSpec(shape, lambda i: (0, 0))
    tok = lambda n: pl.BlockSpec((tm, n), lambda i: (i, 0))
    feat = lambda n: pl.BlockSpec((n, tm), lambda i: (0, i))
    return pl.pallas_call(
        functools.partial(_proj_kernel, tk=tk),
        grid=(t // tm,),
        in_specs=[tok(d), const((d, W_ALL)), const((1, W_ALL)), const((1, W_ALL)),
                  tok(LANES), tok(LANES), tok(LANES)],
        out_specs=[feat(HW), tok(HW), pl.BlockSpec((tm // tk, HW, tk), lambda i: (i, 0, 0)),
                   feat(nqi), tok(IDX_DIM), feat(IDX_HEADS), tok(LANES), tok(W_REST)],
        out_shape=[jax.ShapeDtypeStruct((HW, t), MXU_DTYPE),
                   jax.ShapeDtypeStruct((t, HW), MXU_DTYPE),
                   jax.ShapeDtypeStruct((t // tk, HW, tk), MXU_DTYPE),
                   jax.ShapeDtypeStruct((nqi, t), MXU_DTYPE),
                   jax.ShapeDtypeStruct((t, IDX_DIM), MXU_DTYPE),
                   jax.ShapeDtypeStruct((IDX_HEADS, t), F32),
                   jax.ShapeDtypeStruct((t, LANES), F32),
                   jax.ShapeDtypeStruct((t, W_REST), MXU_DTYPE)],
        compiler_params=_cparams("arbitrary"),
        name="in_proj",
    )(h2d, w_all, b_all, s_all, cos, sa, sb)


def _pack_in_weights(w_in, b_in):
    offs = np.cumsum((0,) + SPLIT_SIZES)
    part = lambda a, i: a[..., offs[i]:offs[i + 1]]
    padw = lambda a, n: jnp.pad(a, [(0, 0)] * (a.ndim - 1) + [(0, n - a.shape[-1])])

    def heads3(a, width):
        return [padw(a[..., j * width:(j + 1) * width], HW) for j in range(3)]

    def build(a):
        qa, ka, va, qi, ki, wi, qkv_b, qkv_c, f_c = (part(a, i) for i in range(9))
        small = padw(jnp.concatenate([wi, f_c], axis=-1), LANES)
        cols = ([qa, ka, qi, padw(ki, LANES), small, va]
                + heads3(qkv_b, WIDTH_B) + heads3(qkv_c, WIDTH_C))
        return jnp.concatenate(cols, axis=-1)

    scale = np.ones((1, W_ALL), np.float32)
    qs = HEAD_DIM ** -0.5
    scale[:, OFF_QK:OFF_QK + HW] = qs
    scale[:, OFF_SM + SM_W:OFF_SM + SM_W + IDX_HEADS] = IDX_HEADS ** -0.5 * IDX_DIM ** -0.5
    scale[:, OFF_REST:OFF_REST + HW] = qs
    scale[:, OFF_REST + 3 * HW:OFF_REST + 4 * HW] = qs
    w_all = build(w_in).astype(MXU_DTYPE)
    b_all = build(b_in[None, :])
    gates_w = part(w_in, 9).astype(MXU_DTYPE)
    gates_b = part(b_in[None, :], 9)
    return w_all, b_all, jnp.asarray(scale), gates_w, gates_b


def _cumsum_kernel(sm_ref, tri_ref, col_ref, row_ref, carry_ref):
    @pl.when(pl.program_id(1) == 0)
    def _():
        carry_ref[...] = jnp.zeros_like(carry_ref)

    x = sm_ref[0]
    lf = jnp.minimum(x, 0.0) - jnp.log(1.0 + jnp.exp(-jnp.abs(x)))
    cs = jnp.dot(tri_ref[...], lf, precision=lax.Precision.HIGHEST,
                 preferred_element_type=F32) + carry_ref[0:1, :]
    col_ref[0] = cs
    row_ref[0, 0] = cs.T[SM_F:SM_F + SUBLANES, :]
    carry_ref[...] = jnp.broadcast_to(cs[-1:, :], carry_ref.shape)


def _forget_cumsum(sm, tk):
    b, s, _ = sm.shape
    tri = jnp.asarray(np.tril(np.ones((tk, tk), np.float32)))
    return pl.pallas_call(
        _cumsum_kernel,
        grid=(b, s // tk),
        in_specs=[pl.BlockSpec((1, tk, LANES), lambda bi, i: (bi, i, 0)),
                  pl.BlockSpec((tk, tk), lambda bi, i: (0, 0))],
        out_specs=[pl.BlockSpec((1, tk, LANES), lambda bi, i: (bi, i, 0)),
                   pl.BlockSpec((1, 1, SUBLANES, tk), lambda bi, i: (bi, i, 0, 0))],
        out_shape=[jax.ShapeDtypeStruct((b, s, LANES), F32),
                   jax.ShapeDtypeStruct((b, s // tk, SUBLANES, tk), F32)],
        scratch_shapes=[pltpu.VMEM((SUBLANES, LANES), F32)],
        compiler_params=_cparams("arbitrary", "arbitrary"),
        name="forget_cumsum",
    )(sm, tri)


def _dsa_kernel(qit_ref, ki_ref, wt_ref, qt_ref, k_ref, vt_ref, tri_ref, o_ref,
                keys_ref, qm_ref, acc_ref, s_ref, *, tb, topk):
    i = pl.program_id(1)
    nkb = i + 1
    q_chunk = (i * tb + lax.broadcasted_iota(jnp.int32, (1, tb), 1)) // CHUNK

    zero = jnp.zeros((HEAD_DIM, tb), qm_ref.dtype)
    for h in range(N_HEADS_A):
        qh = qt_ref[h * HEAD_DIM:(h + 1) * HEAD_DIM, :]
        qm_ref[h] = jnp.concatenate([qh, zero] if h % 2 == 0 else [zero, qh], axis=0)

    def store_scores(kb, diagonal):
        k_blk = ki_ref[0, pl.ds(pl.multiple_of(kb * tb, tb), tb), :]
        acc = jnp.zeros((tb, tb), F32)
        for h in range(IDX_HEADS):
            s = _dot(k_blk, qit_ref[h * IDX_DIM:(h + 1) * IDX_DIM, :])
            acc = acc + wt_ref[h:h + 1, :] * jnp.maximum(s, 0.0)
        if diagonal:
            k_chunk = (kb * tb + lax.broadcasted_iota(jnp.int32, (tb, 1), 0)) // CHUNK
            acc = jnp.where(k_chunk <= q_chunk, acc, -jnp.inf)
        bits = lax.bitcast_convert_type(acc, jnp.int32)
        keys_ref[kb] = bits ^ ((bits >> 31) & jnp.int32(0x7FFFFFFF))

    def score_body(kb, carry):
        store_scores(kb, False)
        return carry

    lax.fori_loop(0, i, score_body, 0)
    store_scores(i, True)

    kcap = jnp.minimum(topk, (q_chunk + 1) * CHUNK).astype(F32)

    def count(pred):
        def body(kb, acc):
            hit = jnp.where(pred(keys_ref[kb]), 1.0, 0.0)
            return acc + jnp.sum(hit.reshape(tb // COUNT_ROWS, COUNT_ROWS, tb), axis=0)
        acc = lax.fori_loop(0, nkb, body, jnp.zeros((COUNT_ROWS, tb), F32))
        return jnp.sum(acc, axis=0, keepdims=True)

    def bit_body(t, prefix):
        cand_u = prefix | (jnp.int32(1) << (31 - t))
        cand_s = cand_u ^ jnp.int32(INT_MIN)
        return jnp.where(count(lambda blk: blk >= cand_s) >= kcap, cand_u, prefix)

    prefix = lax.fori_loop(0, 32, bit_body, jnp.zeros((1, tb), jnp.int32))
    thr = prefix ^ jnp.int32(INT_MIN)
    n_gt = count(lambda blk: blk > thr)
    n_eq = count(lambda blk: blk >= thr) - n_gt
    need = kcap - n_gt
    no_cut = jnp.min(jnp.where(need == n_eq, 1, 0)) == 1

    acc_ref[...] = jnp.zeros(acc_ref.shape, F32)
    row = lambda v: jnp.full((1, tb), v, F32)

    def attend(ranked):
        def body(kb, carry):
            tie_seen, ms, ls = carry
            blk = keys_ref[kb]
            if ranked:
                eq = blk == thr
                rank = _dot(tri_ref[...], jnp.where(eq, 1.0, 0.0).astype(tri_ref.dtype))
                keep = (blk > thr) | (eq & (rank + tie_seen <= need))
                tie_seen = tie_seen + rank[tb - 1:tb, :]
            else:
                keep = blk >= thr
            bias = jnp.where(keep, 0.0, -jnp.inf)
            start = pl.multiple_of(kb * tb, tb)

            def qk(h):
                return _dot(k_ref[0, pl.ds(start, tb), (h // 2) * LANES:(h // 2 + 1) * LANES], qm_ref[h])

            new_ms, new_ls, alphas, pvs = [], [], [], []
            s_ref[0] = qk(0) + bias
            for h in range(N_HEADS_A):
                if h + 1 < N_HEADS_A:
                    s_ref[(h + 1) % 2] = qk(h + 1) + bias
                s = s_ref[h % 2]
                m_new = jnp.maximum(ms[h], jnp.max(s, axis=0, keepdims=True))
                alphas.append(jnp.exp(ms[h] - m_new))
                pe = jnp.exp(s - m_new)
                new_ls.append(alphas[h] * ls[h] + jnp.sum(pe, axis=0, keepdims=True))
                new_ms.append(m_new)
                pvs.append(_dot(vt_ref[0, kb, h * HEAD_DIM:(h + 1) * HEAD_DIM, :], pe.astype(vt_ref.dtype)))
                if h > 0:
                    acc_ref[h - 1] = acc_ref[h - 1] * alphas[h - 1] + pvs[h - 1]
            h = N_HEADS_A - 1
            acc_ref[h] = acc_ref[h] * alphas[h] + pvs[h]
            return tie_seen, tuple(new_ms), tuple(new_ls)

        init = (row(0.0), (row(NEG_BIG),) * N_HEADS_A, (row(0.0),) * N_HEADS_A)
        return lax.fori_loop(0, nkb, body, init)[2]

    ls = lax.cond(no_cut, lambda: attend(False), lambda: attend(True))
    out_t = jnp.concatenate([acc_ref[h] / ls[h] for h in range(N_HEADS_A)], axis=0)
    o_ref[0] = out_t.T.astype(o_ref.dtype)


def _dsa_attention(qt, k, vt, qit, ki, wt, topk, tb):
    b, s, _ = k.shape
    nq = s // tb
    nqi = IDX_HEADS * IDX_DIM
    tri = jnp.asarray(np.tril(np.ones((tb, tb), np.float32))).astype(MXU_DTYPE)
    feat = lambda n: pl.BlockSpec((n, tb), lambda bi, i: (0, bi * nq + i))
    full = lambda n: pl.BlockSpec((1, s, n), lambda bi, i: (bi, 0, 0))
    return pl.pallas_call(
        functools.partial(_dsa_kernel, tb=tb, topk=topk),
        grid=(b, nq),
        in_specs=[feat(nqi), full(IDX_DIM), feat(IDX_HEADS), feat(HW), full(HW),
                  pl.BlockSpec((1, nq, HW, tb), lambda bi, i: (bi, 0, 0, 0)),
                  pl.BlockSpec((tb, tb), lambda bi, i: (0, 0))],
        out_specs=pl.BlockSpec((1, tb, HW), lambda bi, i: (bi, i, 0)),
        out_shape=jax.ShapeDtypeStruct((b, s, HW), MXU_DTYPE),
        scratch_shapes=[pltpu.VMEM((nq, tb, tb), jnp.int32),
                        pltpu.VMEM((N_HEADS_A, LANES, tb), MXU_DTYPE),
                        pltpu.VMEM((N_HEADS_A, HEAD_DIM, tb), F32),
                        pltpu.VMEM((2, tb, tb), F32)],
        compiler_params=_cparams("arbitrary", "arbitrary"),
        name="dsa_attention",
    )(qit, ki, wt, qt, k, vt.reshape(b, nq, HW, tb), tri)


def _even_lanes(rows):
    return lax.broadcasted_iota(jnp.int32, (rows, LANES), 1) < HEAD_DIM


def _store_masked_heads(dst_ref, src, n_heads):
    even = _even_lanes(src.shape[0])
    zero = jnp.zeros((src.shape[0], LANES), src.dtype)
    for h in range(n_heads):
        slab = src[:, (h // 2) * LANES:(h // 2 + 1) * LANES]
        dst_ref[h] = jnp.where(even, slab, zero) if h % 2 == 0 else jnp.where(even, zero, slab)


def _rep(col):
    return jnp.broadcast_to(col, (col.shape[0], LANES))


def _flash_step(qm_ref, m_ref, l_ref, acc_ref, k_blk, v_blk, bias_fn, n_heads):
    tq = acc_ref.shape[1]
    nslab = k_blk.shape[0] // LANES
    even = _even_lanes(tq)
    for p in range((n_heads + 1) // 2):
        kp = k_blk[:, p * LANES:(p + 1) * LANES]
        vp = v_blk[:, p * LANES:(p + 1) * LANES]
        pvs, alphas = [], []
        for h in range(2 * p, min(2 * p + 2, n_heads)):
            s = bias_fn(h, _dot_nt(qm_ref[h], kp))
            m_old = m_ref[h]
            m_new = jnp.maximum(m_old, _rep(jnp.max(s, axis=1, keepdims=True)))
            alpha = jnp.exp(m_old - m_new)
            pe = jnp.concatenate(
                [jnp.exp(s[:, c * LANES:(c + 1) * LANES] - m_new) for c in range(nslab)], axis=1)
            l_ref[h] = alpha * l_ref[h] + _rep(jnp.sum(pe, axis=1, keepdims=True))
            m_ref[h] = m_new
            pvs.append(_dot(pe.astype(vp.dtype), vp))
            alphas.append(alpha)
        if len(pvs) == 2:
            acc_ref[p] = acc_ref[p] * jnp.where(even, alphas[0], alphas[1]) + jnp.where(even, pvs[0], pvs[1])
        else:
            acc_ref[p] = acc_ref[p] * alphas[0] + pvs[0]


def _flash_init(m_ref, l_ref, acc_ref):
    m_ref[...] = jnp.full(m_ref.shape, NEG_BIG, F32)
    l_ref[...] = jnp.zeros(l_ref.shape, F32)
    acc_ref[...] = jnp.zeros(acc_ref.shape, F32)


def _flash_finish(o_ref, l_ref, acc_ref, n_heads):
    even = _even_lanes(acc_ref.shape[1])
    for p in range(acc_ref.shape[0]):
        l = jnp.where(even, l_ref[2 * p], l_ref[2 * p + 1]) if 2 * p + 1 < n_heads else l_ref[2 * p]
        o_ref[0, :, p * LANES:(p + 1) * LANES] = (acc_ref[p] / l).astype(o_ref.dtype)


def _band_kernel(q_ref, k_ref, v_ref, bt_ref, o_ref, qm_ref, *, tq):
    i = pl.program_id(1)
    win = tq + BAND_PAD
    start = pl.multiple_of(i * tq, tq)
    _store_masked_heads(qm_ref, q_ref[0], N_HEADS_B)
    k_win = k_ref[0, pl.ds(start, win), :]
    v_win = v_ref[0, pl.ds(start, win), :]
    col = lax.broadcasted_iota(jnp.int32, (tq, win), 1)
    in_seq = col >= BAND_PAD - i * tq
    even = _even_lanes(tq)
    for p in range(HW // LANES):
        kp = k_win[:, p * LANES:(p + 1) * LANES]
        vp = v_win[:, p * LANES:(p + 1) * LANES]
        outs = []
        for h in range(2 * p, min(2 * p + 2, N_HEADS_B)):
            s = jnp.where(in_seq, _dot_nt(qm_ref[h], kp) + bt_ref[h], NEG_BIG)
            pe = jnp.exp(s - jnp.max(s, axis=1, keepdims=True))
            outs.append(_dot(pe.astype(vp.dtype), vp) / jnp.sum(pe, axis=1, keepdims=True))
        o = jnp.where(even, outs[0], outs[1]) if len(outs) == 2 else outs[0]
        o_ref[0, :, p * LANES:(p + 1) * LANES] = o.astype(o_ref.dtype)


def _band_bias_table(rel_bias, tq):
    win = tq + BAND_PAD
    r = np.arange(tq)[:, None]
    c = np.arange(win)[None, :]
    dist = np.clip(r + BAND_PAD - c, -REL_CLIP, REL_CLIP) + REL_CLIP
    back = r // CHUNK + PREV_CHUNKS - c // CHUNK
    in_band = (back >= 0) & (back <= PREV_CHUNKS)
    return jnp.where(jnp.asarray(in_band)[None], rel_bias[:, dist], NEG_BIG)


def _band_attention(rest, rel_bias):
    b, s, _ = rest.shape
    tq = min(128, s)
    win = tq + BAND_PAD
    kv = jnp.pad(rest[:, :, HW:3 * HW], ((0, 0), (BAND_PAD, 0), (0, 0)))
    bt = _band_bias_table(rel_bias, tq)
    return pl.pallas_call(
        functools.partial(_band_kernel, tq=tq),
        grid=(b, s // tq),
        in_specs=[pl.BlockSpec((1, tq, HW), lambda bi, i: (bi, i, 0)),
                  pl.BlockSpec((1, s + BAND_PAD, HW), lambda bi, i: (bi, 0, 0)),
                  pl.BlockSpec((1, s + BAND_PAD, HW), lambda bi, i: (bi, 0, 1)),
                  pl.BlockSpec((N_HEADS_B, tq, win), lambda bi, i: (0, 0, 0))],
        out_specs=pl.BlockSpec((1, tq, HW), lambda bi, i: (bi, i, 0)),
        out_shape=jax.ShapeDtypeStruct((b, s, HW), MXU_DTYPE),
        scratch_shapes=[pltpu.VMEM((N_HEADS_B, tq, LANES), MXU_DTYPE)],
        compiler_params=_cparams("arbitrary", "arbitrary"),
        name="band_attention",
    )(rest, kv, kv, bt)


def _forget_kernel(q_ref, k_ref, v_ref, fcol_ref, frow_ref, o_ref,
                   qm_ref, fq_ref, m_ref, l_ref, acc_ref, *, tq):
    i = pl.program_id(1)
    _store_masked_heads(qm_ref, q_ref[0], N_HEADS_C)
    fcol = fcol_ref[0]
    for h in range(N_HEADS_C):
        fq_ref[h] = _rep(fcol[:, SM_F + h:SM_F + h + 1])
    _flash_init(m_ref, l_ref, acc_ref)
    nslab = tq // LANES

    def step(kb, causal):
        start = pl.multiple_of(kb * tq, tq)
        f_k = frow_ref[0, kb]
        if causal:
            tri = (lax.broadcasted_iota(jnp.int32, (tq, tq), 1)
                   <= lax.broadcasted_iota(jnp.int32, (tq, tq), 0))

        def bias_fn(h, s):
            fq = fq_ref[h]
            out = jnp.concatenate(
                [s[:, c * LANES:(c + 1) * LANES] + (fq - f_k[h:h + 1, c * LANES:(c + 1) * LANES])
                 for c in range(nslab)], axis=1)
            return jnp.where(tri, out, -jnp.inf) if causal else out

        _flash_step(qm_ref, m_ref, l_ref, acc_ref,
                    k_ref[0, pl.ds(start, tq), :], v_ref[0, pl.ds(start, tq), :],
                    bias_fn, N_HEADS_C)

    def body(kb, carry):
        step(kb, False)
        return carry

    lax.fori_loop(0, i, body, 0)
    step(i, True)
    _flash_finish(o_ref, l_ref, acc_ref, N_HEADS_C)


def _forget_attention(rest, fcol, frow, tq):
    b, s, _ = rest.shape
    return pl.pallas_call(
        functools.partial(_forget_kernel, tq=tq),
        grid=(b, s // tq),
        in_specs=[pl.BlockSpec((1, tq, HW), lambda bi, i: (bi, i, 3)),
                  pl.BlockSpec((1, s, HW), lambda bi, i: (bi, 0, 4)),
                  pl.BlockSpec((1, s, HW), lambda bi, i: (bi, 0, 5)),
                  pl.BlockSpec((1, tq, LANES), lambda bi, i: (bi, i, 0)),
                  pl.BlockSpec((1, s // tq, SUBLANES, tq), lambda bi, i: (bi, 0, 0, 0))],
        out_specs=pl.BlockSpec((1, tq, HW), lambda bi, i: (bi, i, 0)),
        out_shape=jax.ShapeDtypeStruct((b, s, HW), MXU_DTYPE),
        scratch_shapes=[pltpu.VMEM((N_HEADS_C, tq, LANES), MXU_DTYPE),
                        pltpu.VMEM((N_HEADS_C, tq, LANES), F32),
                        pltpu.VMEM((N_HEADS_C, tq, LANES), F32),
                        pltpu.VMEM((N_HEADS_C, tq, LANES), F32),
                        pltpu.VMEM((HW // LANES, tq, LANES), F32)],
        compiler_params=_cparams("arbitrary", "arbitrary"),
        name="forget_attention",
    )(rest, rest, rest, fcol, frow)


def _merge_kernel(x_ref, h_ref, oa_ref, ob_ref, oc_ref, wg_ref, bg_ref, wb_ref, wo_ref, g_ref, o_ref):
    h = h_ref[0]
    d = h.shape[1]
    y = None
    for br, o_br in enumerate((oa_ref, ob_ref, oc_ref)):
        gate = _sigmoid(_dot(h, wg_ref[:, br * d:(br + 1) * d]) + bg_ref[:, br * d:(br + 1) * d])
        term = gate * _dot(o_br[0], wb_ref[br])
        y = term if y is None else y + term
    o_ref[0] = x_ref[0] + g_ref[0] * _dot(y.astype(wo_ref.dtype), wo_ref[...])


def _merge(x, h, o_a, o_b, o_c, gates_w, gates_b, wb, w_out, mod, g_blk):
    b, s, d = x.shape
    tm = min(s, 512)
    tok = lambda n: pl.BlockSpec((1, tm, n), lambda bi, i: (bi, i, 0))
    const2 = lambda shape: pl.BlockSpec(shape, lambda bi, i: (0, 0))
    return pl.pallas_call(
        _merge_kernel,
        grid=(b, s // tm),
        in_specs=[tok(d), tok(d), tok(HW), tok(HW), tok(HW),
                  const2((d, N_BRANCH * d)), const2((1, N_BRANCH * d)),
                  pl.BlockSpec((N_BRANCH, HW, d), lambda bi, i: (0, 0, 0)),
                  const2((d, d)),
                  pl.BlockSpec((1, 1, d), lambda bi, i: (bi, 0, g_blk))],
        out_specs=tok(d),
        out_shape=jax.ShapeDtypeStruct((b, s, d), F32),
        compiler_params=_cparams("arbitrary", "arbitrary"),
        name="gated_merge",
    )(x, h, o_a, o_b, o_c, gates_w, gates_b, wb, w_out, mod)


def _ffn_kernel(x_ref, h_ref, wg_ref, wu_ref, wo_ref, g_ref, o_ref, acc_ref):
    j = pl.program_id(2)

    @pl.when(j == 0)
    def _():
        acc_ref[...] = jnp.zeros_like(acc_ref)

    h = h_ref[0]
    g = _dot(h, wg_ref[...])
    u = _dot(h, wu_ref[...])
    act = (g * _sigmoid(g)) * u
    acc_ref[...] += _dot(act.astype(wo_ref.dtype), wo_ref[...])

    @pl.when(j == pl.num_programs(2) - 1)
    def _():
        o_ref[0] = x_ref[0] + g_ref[0] * acc_ref[...]


def _ffn(x, h, w_in, w_out, mod, g_blk):
    b, s, d = x.shape
    tm = min(s, 1024)
    tf = 256
    nf = FFN_HIDDEN // tf
    tok = lambda: pl.BlockSpec((1, tm, d), lambda bi, i, j: (bi, i, 0))
    return pl.pallas_call(
        _ffn_kernel,
        grid=(b, s // tm, nf),
        in_specs=[tok(), tok(),
                  pl.BlockSpec((d, tf), lambda bi, i, j: (0, j)),
                  pl.BlockSpec((d, tf), lambda bi, i, j: (0, nf + j)),
                  pl.BlockSpec((tf, d), lambda bi, i, j: (j, 0)),
                  pl.BlockSpec((1, 1, d), lambda bi, i, j: (bi, 0, g_blk))],
        out_specs=tok(),
        out_shape=jax.ShapeDtypeStruct((b, s, d), F32),
        scratch_shapes=[pltpu.VMEM((tm, d), F32)],
        compiler_params=_cparams("arbitrary", "arbitrary", "arbitrary"),
        name="swiglu_ffn",
    )(x, h, w_in, w_in, w_out, mod)


def _pack_branch_weights(w_branch):
    pad = lambda a: jnp.pad(a, ((0, HW - a.shape[0]), (0, 0)))
    return jnp.stack([pad(w_branch[:WIDTH_A]),
                      pad(w_branch[WIDTH_A:WIDTH_A + WIDTH_B]),
                      pad(w_branch[WIDTH_A + WIDTH_B:])]).astype(MXU_DTYPE)


def kernel(x, c, positions, ada_w, ada_b, norm1_g, w_in, b_in, rel_bias, w_branch, w_out,
           norm2_g, w_ffn_in, w_ffn_out, final_g):
    b, s, d = x.shape
    depth = ada_w.shape[0]
    topk = min(TOPK_MAX, s // 4)
    t_dsa = min(DSA_BLOCK, s)
    t_forget = min(FORGET_BLOCK, s)
    cos, sa, sb = _rope_tables(positions)
    mod_all = _ada_mod(c, ada_w, ada_b)
    for l in range(depth):
        mod = mod_all[l].reshape(b, 1, 6 * d)
        w_all, b_all, s_all, gates_w, gates_b = _pack_in_weights(w_in[l], b_in[l])
        h = _norm(x, norm1_g[l], mod, 1, 0, MXU_DTYPE)
        qt, k_a, vt, qit, ki, wt, sm, rest = _proj(h.reshape(b * s, d), w_all, b_all, s_all,
                                                   cos, sa, sb, t_dsa)
        k_a, ki, sm, rest = (a.reshape(b, s, a.shape[-1]) for a in (k_a, ki, sm, rest))
        o_a = _dsa_attention(qt, k_a, vt, qit, ki, wt, topk, t_dsa)
        o_b = _band_attention(rest, rel_bias[l])
        fcol, frow = _forget_cumsum(sm, t_forget)
        o_c = _forget_attention(rest, fcol, frow, t_forget)
        x = _merge(x, h, o_a, o_b, o_c, gates_w, gates_b, _pack_branch_weights(w_branch[l]),
                   w_out[l].astype(MXU_DTYPE), mod, 2)
        h = _norm(x, norm2_g[l], mod, 4, 3, MXU_DTYPE)
        x = _ffn(x, h, w_ffn_in[l].astype(MXU_DTYPE), w_ffn_out[l].astype(MXU_DTYPE), mod, 5)
    return _norm(x, final_g, None, 0, 0, x.dtype)
```

```python
import functools
import math

import numpy as np
import jax
import jax.numpy as jnp
from jax import lax
from jax.experimental import pallas as pl
from jax.experimental.pallas import tpu as pltpu

D_MODEL = 1024
CHUNK = 64
HEAD_DIM = 64
ROT_DIM = HEAD_DIM // 4
ROPE_THETA = 500000.0
N_HEADS_A = 6
N_HEADS_B = 5
N_HEADS_C = 5
WIDTH_A = N_HEADS_A * HEAD_DIM
WIDTH_B = N_HEADS_B * HEAD_DIM
WIDTH_C = N_HEADS_C * HEAD_DIM
IDX_HEADS = 8
IDX_DIM = 64
TOPK_MAX = 256
PREV_CHUNKS = 8
REL_CLIP = 128
N_BRANCH = 3
FFN_HIDDEN = -(-8 * D_MODEL // (3 * 256)) * 256
RMS_EPS = 1e-6
SPLIT_SIZES = (WIDTH_A, WIDTH_A, WIDTH_A, IDX_HEADS * IDX_DIM, IDX_DIM, IDX_HEADS,
               3 * WIDTH_B, 3 * WIDTH_C, N_HEADS_C, N_BRANCH * D_MODEL)

LANES = 128
SUBLANES = 8
PACKED_ROWS = 16
HW = 384
BAND_PAD = PREV_CHUNKS * CHUNK
VMEM_LIMIT = 56 * 1024 * 1024

MXU_DTYPE = jnp.bfloat16
F32 = jnp.float32
NEG_BIG = -1e30
INT_MIN = -2 ** 31
LOG2E = math.log2(math.e)

OFF_QK = 0
OFF_QI = 2 * HW
OFF_KI = OFF_QI + IDX_HEADS * IDX_DIM
OFF_SM = OFF_KI + LANES
OFF_VA = OFF_SM + LANES
OFF_B = OFF_VA + HW
OFF_C = OFF_B + 3 * HW
W_ALL = OFF_C + 3 * HW
SM_W = 0
SM_F = 8
Q_BLOCK = 256
K_BLOCK = 512
V_ROWS = HEAD_DIM + PACKED_ROWS
COUNT_ROWS = 4 * SUBLANES


def _cparams(*sem):
    return pltpu.CompilerParams(dimension_semantics=sem, vmem_limit_bytes=VMEM_LIMIT)


def _dot(a, b):
    return jnp.dot(a, b, preferred_element_type=F32)


def _dot_nt(a, b):
    return lax.dot_general(a, b, (((1,), (1,)), ((), ())), preferred_element_type=F32)


def _sigmoid(x):
    return 1.0 / (1.0 + jnp.exp(-x))


def _rope_kernel(pos_ref, invf_ref, ma_ref, mb_ref, cos_ref, sa_ref, sb_ref):
    ang = pos_ref[...].astype(F32) * invf_ref[...]
    s = jnp.sin(ang)
    cos_ref[...] = jnp.cos(ang)
    sa_ref[...] = s * ma_ref[...]
    sb_ref[...] = s * mb_ref[...]


def _rope_tables(positions):
    t = positions.size
    tm = min(t, 1024)
    half = ROT_DIM // 2
    d = np.arange(LANES) % HEAD_DIM
    inv = (ROPE_THETA ** (-(2.0 * (d % half)) / ROT_DIM)).astype(np.float32)
    invf = np.where(d < ROT_DIM, inv, 0.0).astype(np.float32)[None, :]
    ma = np.where(d < half, -1.0, 0.0).astype(np.float32)[None, :]
    mb = np.where((d >= half) & (d < ROT_DIM), 1.0, 0.0).astype(np.float32)[None, :]
    row = pl.BlockSpec((1, LANES), lambda i: (0, 0))
    tab = pl.BlockSpec((tm, LANES), lambda i: (i, 0))
    return pl.pallas_call(
        _rope_kernel,
        grid=(t // tm,),
        in_specs=[pl.BlockSpec((tm, 1), lambda i: (i, 0)), row, row, row],
        out_specs=[tab, tab, tab],
        out_shape=[jax.ShapeDtypeStruct((t, LANES), F32)] * 3,
        compiler_params=_cparams("arbitrary"),
        name="rope_tables",
    )(positions.reshape(t, 1), jnp.asarray(invf), jnp.asarray(ma), jnp.asarray(mb))


def _ada_kernel(c_ref, w_ref, b_ref, o_ref):
    c = c_ref[...]
    cond = c * _sigmoid(c)
    o_ref[0] = jnp.dot(cond, w_ref[0], precision=lax.Precision.HIGHEST,
                       preferred_element_type=F32) + b_ref[0]


def _ada_mod(c, ada_w, ada_b):
    depth, d, n = ada_w.shape
    b = c.shape[0]
    rows = -(-b // SUBLANES) * SUBLANES
    cp = jnp.pad(c, ((0, rows - b), (0, 0)))
    tn = 1024
    out = pl.pallas_call(
        _ada_kernel,
        grid=(depth, n // tn),
        in_specs=[pl.BlockSpec((rows, d), lambda l, j: (0, 0)),
                  pl.BlockSpec((1, d, tn), lambda l, j: (l, 0, j)),
                  pl.BlockSpec((1, 1, tn), lambda l, j: (l, 0, j))],
        out_specs=pl.BlockSpec((1, rows, tn), lambda l, j: (l, 0, j)),
        out_shape=jax.ShapeDtypeStruct((depth, rows, n), F32),
        compiler_params=_cparams("arbitrary", "arbitrary"),
        name="ada_mod",
    )(cp, ada_w, ada_b.reshape(depth, 1, n))
    return out[:, :b]


def _norm_kernel(x_ref, g_ref, *rest, modulate):
    x = x_ref[0]
    y = x * lax.rsqrt(jnp.mean(x * x, axis=-1, keepdims=True) + RMS_EPS) * g_ref[...]
    if modulate:
        sc_ref, sh_ref, o_ref = rest
        y = y * (1.0 + sc_ref[0]) + sh_ref[0]
    else:
        (o_ref,) = rest
    o_ref[0] = y.astype(o_ref.dtype)


def _norm(x, g, mod, sc_blk, sh_blk, out_dtype):
    b, s, d = x.shape
    tm = min(s, 512)
    xspec = pl.BlockSpec((1, tm, d), lambda bi, i: (bi, i, 0))
    in_specs = [xspec, pl.BlockSpec((1, d), lambda bi, i: (0, 0))]
    args = [x, g.reshape(1, d)]
    if mod is not None:
        in_specs += [pl.BlockSpec((1, 1, d), lambda bi, i: (bi, 0, sc_blk)),
                     pl.BlockSpec((1, 1, d), lambda bi, i: (bi, 0, sh_blk))]
        args += [mod, mod]
    return pl.pallas_call(
        functools.partial(_norm_kernel, modulate=mod is not None),
        grid=(b, s // tm),
        in_specs=in_specs,
        out_specs=xspec,
        out_shape=jax.ShapeDtypeStruct((b, s, d), out_dtype),
        compiler_params=_cparams("arbitrary", "arbitrary"),
        name="norm_mod" if mod is not None else "norm_final",
    )(*args)


def _proj_kernel(h_ref, w_ref, b_ref, s_ref, cos_ref, sa_ref, sb_ref,
                 qta_ref, ka_ref, vta_ref, qit_ref, ki_ref, wt_ref, sm_ref,
                 b_out_ref, qtc_ref, kc_ref, vtc_ref, *, tk):
    h = h_ref[...]
    tm = h.shape[0]

    def seg(a, n):
        return (_dot(h, w_ref[:, a:a + n]) + b_ref[:, a:a + n]) * s_ref[:, a:a + n]

    def rope(a, n):
        r = seg(a, n)
        cos, sa, sb = cos_ref[...], sa_ref[...], sb_ref[...]
        out = []
        for c in range(n // LANES):
            x = r[:, c * LANES:(c + 1) * LANES]
            out.append(x * cos + pltpu.roll(x, LANES - ROT_DIM // 2, 1) * sa
                       + pltpu.roll(x, ROT_DIM // 2, 1) * sb)
        return jnp.concatenate(out, axis=1)

    def store_vt(vt_ref, v, n_heads):
        ones = jnp.ones((V_ROWS - HEAD_DIM, tk), F32)
        for j in range(tm // tk):
            vt = v[j * tk:(j + 1) * tk, :].T
            rows = []
            for hd in range(n_heads):
                rows += [vt[hd * HEAD_DIM:(hd + 1) * HEAD_DIM, :], ones]
            vt_ref[j] = jnp.concatenate(rows, axis=0).astype(vt_ref.dtype)

    qta_ref[...] = rope(OFF_QK, HW).T.astype(qta_ref.dtype)
    ka_ref[...] = rope(OFF_QK + HW, HW).astype(ka_ref.dtype)
    qit_ref[...] = rope(OFF_QI, IDX_HEADS * IDX_DIM).T.astype(qit_ref.dtype)
    ki_ref[...] = rope(OFF_KI, LANES)[:, :IDX_DIM].astype(ki_ref.dtype)
    sm = seg(OFF_SM, LANES)
    sm_ref[...] = sm
    wt_ref[...] = sm.T[SM_W:SM_W + IDX_HEADS, :]
    store_vt(vta_ref, seg(OFF_VA, HW), N_HEADS_A)
    for c in range(3):
        b_out_ref[:, c * HW:(c + 1) * HW] = seg(OFF_B + c * HW, HW).astype(b_out_ref.dtype)
    qtc_ref[...] = seg(OFF_C, HW).T.astype(qtc_ref.dtype)
    kc_ref[...] = seg(OFF_C + HW, HW).astype(kc_ref.dtype)
    store_vt(vtc_ref, seg(OFF_C + 2 * HW, HW), N_HEADS_C)


def _proj(h2d, w_all, b_all, s_all, cos, sa, sb, tk):
    t, d = h2d.shape
    tm = max(min(t, 512), tk)
    nqi = IDX_HEADS * IDX_DIM
    const = lambda shape: pl.BlockSpec(shape, lambda i: (0, 0))
    tok = lambda n: pl.BlockSpec((tm, n), lambda i: (i, 0))
    feat = lambda n: pl.BlockSpec((n, tm), lambda i: (0, i))
    vts = lambda nh: pl.BlockSpec((tm // tk, nh * V_ROWS, tk), lambda i: (i, 0, 0))
    act = lambda shape: jax.ShapeDtypeStruct(shape, MXU_DTYPE)
    return pl.pallas_call(
        functools.partial(_proj_kernel, tk=tk),
        grid=(t // tm,),
        in_specs=[tok(d), const((d, W_ALL)), const((1, W_ALL)), const((1, W_ALL)),
                  tok(LANES), tok(LANES), tok(LANES)],
        out_specs=[feat(HW), tok(HW), vts(N_HEADS_A), feat(nqi), tok(IDX_DIM), feat(IDX_HEADS),
                   tok(LANES), tok(3 * HW), feat(HW), tok(HW), vts(N_HEADS_C)],
        out_shape=[act((HW, t)), act((t, HW)), act((t // tk, N_HEADS_A * V_ROWS, tk)),
                   act((nqi, t)), act((t, IDX_DIM)),
                   jax.ShapeDtypeStruct((IDX_HEADS, t), F32),
                   jax.ShapeDtypeStruct((t, LANES), F32),
                   act((t, 3 * HW)), act((HW, t)), act((t, HW)),
                   act((t // tk, N_HEADS_C * V_ROWS, tk))],
        compiler_params=_cparams("arbitrary"),
        name="in_proj",
    )(h2d, w_all, b_all, s_all, cos, sa, sb)


def _pack_in_weights(w_in, b_in):
    offs = np.cumsum((0,) + SPLIT_SIZES)
    part = lambda a, i: a[..., offs[i]:offs[i + 1]]
    padw = lambda a, n: jnp.pad(a, [(0, 0)] * (a.ndim - 1) + [(0, n - a.shape[-1])])

    def heads3(a, width):
        return [padw(a[..., j * width:(j + 1) * width], HW) for j in range(3)]

    def build(a):
        qa, ka, va, qi, ki, wi, qkv_b, qkv_c, f_c = (part(a, i) for i in range(9))
        small = padw(jnp.concatenate([wi, f_c], axis=-1), LANES)
        cols = ([qa, ka, qi, padw(ki, LANES), small, va]
                + heads3(qkv_b, WIDTH_B) + heads3(qkv_c, WIDTH_C))
        return jnp.concatenate(cols, axis=-1)

    scale = np.ones((1, W_ALL), np.float32)
    qs = HEAD_DIM ** -0.5
    scale[:, OFF_QK:OFF_QK + HW] = qs * LOG2E
    scale[:, OFF_SM + SM_W:OFF_SM + SM_W + IDX_HEADS] = IDX_HEADS ** -0.5 * IDX_DIM ** -0.5
    scale[:, OFF_B:OFF_B + HW] = qs
    scale[:, OFF_C:OFF_C + HW] = qs * LOG2E
    w_all = build(w_in).astype(MXU_DTYPE)
    b_all = build(b_in[None, :])
    gates_w = part(w_in, 9).astype(MXU_DTYPE)
    gates_b = part(b_in[None, :], 9)
    return w_all, b_all, jnp.asarray(scale), gates_w, gates_b


def _cumsum_kernel(sm_ref, tri_ref, col_ref, row_ref, carry_ref):
    @pl.when(pl.program_id(1) == 0)
    def _():
        carry_ref[...] = jnp.zeros_like(carry_ref)

    x = sm_ref[0]
    lf = jnp.minimum(x, 0.0) - jnp.log(1.0 + jnp.exp(-jnp.abs(x)))
    cs = jnp.dot(tri_ref[...], lf, precision=lax.Precision.HIGHEST,
                 preferred_element_type=F32) + carry_ref[0:1, :]
    col_ref[0] = cs
    row_ref[0, 0] = cs.T[SM_F:SM_F + SUBLANES, :]
    carry_ref[...] = jnp.broadcast_to(cs[-1:, :], carry_ref.shape)


def _forget_cumsum(sm, tq):
    b, s, _ = sm.shape
    tri = jnp.asarray(np.tril(np.ones((tq, tq), np.float32)))
    return pl.pallas_call(
        _cumsum_kernel,
        grid=(b, s // tq),
        in_specs=[pl.BlockSpec((1, tq, LANES), lambda bi, i: (bi, i, 0)),
                  pl.BlockSpec((tq, tq), lambda bi, i: (0, 0))],
        out_specs=[pl.BlockSpec((1, tq, LANES), lambda bi, i: (bi, i, 0)),
                   pl.BlockSpec((1, 1, SUBLANES, tq), lambda bi, i: (bi, i, 0, 0))],
        out_shape=[jax.ShapeDtypeStruct((b, s, LANES), F32),
                   jax.ShapeDtypeStruct((b, s // tq, SUBLANES, tq), F32)],
        scratch_shapes=[pltpu.VMEM((SUBLANES, LANES), F32)],
        compiler_params=_cparams("arbitrary", "arbitrary"),
        name="forget_cumsum",
    )(sm, tri)


def _store_masked_qt(qm_ref, qt_ref, n_heads):
    zero = jnp.zeros((HEAD_DIM, qm_ref.shape[2]), qm_ref.dtype)
    for h in range(n_heads):
        qh = qt_ref[h * HEAD_DIM:(h + 1) * HEAD_DIM, :]
        qm_ref[h] = jnp.concatenate([qh, zero] if h % 2 == 0 else [zero, qh], axis=0)


def _softmax_heads(n_heads, scores_fn, vt_fn, ms, acc_ref, s_ref):
    new_ms, alphas, pvs = [], [], []
    s_ref[0] = scores_fn(0)
    for h in range(n_heads):
        if h + 1 < n_heads:
            s_ref[(h + 1) % 2] = scores_fn(h + 1)
        s = s_ref[h % 2]
        m_new = jnp.maximum(ms[h], jnp.max(s, axis=0, keepdims=True))
        alphas.append(jnp.exp2(ms[h] - m_new))
        pe = jnp.exp2(s - m_new)
        new_ms.append(m_new)
        vt = vt_fn(h)
        pvs.append(_dot(vt, pe.astype(vt.dtype)))
        if h > 0:
            acc_ref[h - 1] = acc_ref[h - 1] * alphas[h - 1] + pvs[h - 1]
    h = n_heads - 1
    acc_ref[h] = acc_ref[h] * alphas[h] + pvs[h]
    return tuple(new_ms)


def _finish_heads(o_ref, acc_ref, n_heads):
    outs = [acc_ref[h, :HEAD_DIM, :] / acc_ref[h, HEAD_DIM:HEAD_DIM + 1, :] for h in range(n_heads)]
    pad = HW // HEAD_DIM - n_heads
    if pad:
        outs.append(jnp.zeros((pad * HEAD_DIM, acc_ref.shape[2]), F32))
    o_ref[0] = jnp.concatenate(outs, axis=0).T.astype(o_ref.dtype)


def _dsa_kernel(qit_ref, ki_ref, wt_ref, qt_ref, k_ref, vt_ref, tri_ref, o_ref,
                keys_ref, qm_ref, acc_ref, s_ref, *, tq, tk, topk):
    i = pl.program_id(1)
    nkb = (i * tq + tq - 1) // tk + 1
    q_pos = i * tq + lax.broadcasted_iota(jnp.int32, (1, tq), 1)
    q_chunk = q_pos // CHUNK
    _store_masked_qt(qm_ref, qt_ref, N_HEADS_A)

    def store_scores(kb, last):
        k_blk = ki_ref[0, pl.ds(pl.multiple_of(kb * tk, tk), tk), :]
        acc = jnp.zeros((tk, tq), F32)
        for h in range(IDX_HEADS):
            s = _dot(k_blk, qit_ref[h * IDX_DIM:(h + 1) * IDX_DIM, :])
            acc = acc + wt_ref[h:h + 1, :] * jnp.maximum(s, 0.0)
        if last:
            k_chunk = (kb * tk + lax.broadcasted_iota(jnp.int32, (tk, 1), 0)) // CHUNK
            acc = jnp.where(k_chunk <= q_chunk, acc, -jnp.inf)
        bits = lax.bitcast_convert_type(acc, jnp.int32)
        keys_ref[kb] = bits ^ ((bits >> 31) & jnp.int32(0x7FFFFFFF))

    def score_body(kb, carry):
        store_scores(kb, False)
        return carry

    lax.fori_loop(0, nkb - 1, score_body, 0)
    store_scores(nkb - 1, True)

    kcap = jnp.minimum(topk, (q_chunk + 1) * CHUNK).astype(F32)

    def count(pred):
        def body(kb, acc):
            for c in range(tk // LANES):
                hit = jnp.where(pred(keys_ref[kb, c * LANES:(c + 1) * LANES, :]), 1.0, 0.0)
                acc = acc + jnp.sum(hit.reshape(LANES // COUNT_ROWS, COUNT_ROWS, tq), axis=0)
            return acc
        acc = lax.fori_loop(0, nkb, body, jnp.zeros((COUNT_ROWS, tq), F32))
        return jnp.sum(acc, axis=0, keepdims=True)

    def bit_body(t, prefix):
        cand_u = prefix | (jnp.int32(1) << (31 - t))
        cand_s = cand_u ^ jnp.int32(INT_MIN)
        return jnp.where(count(lambda blk: blk >= cand_s) >= kcap, cand_u, prefix)

    prefix = lax.fori_loop(0, 32, bit_body, jnp.zeros((1, tq), jnp.int32))
    thr = prefix ^ jnp.int32(INT_MIN)
    n_gt = count(lambda blk: blk > thr)
    n_eq = count(lambda blk: blk >= thr) - n_gt
    need = kcap - n_gt
    no_cut = jnp.min(jnp.where(need == n_eq, 1, 0)) == 1

    acc_ref[...] = jnp.zeros(acc_ref.shape, F32)
    row = lambda v: jnp.full((1, tq), v, F32)

    def attend(ranked):
        def body(kb, carry):
            tie_seen, ms = carry
            blk = keys_ref[kb]
            if ranked:
                eq = blk == thr
                rank = _dot(tri_ref[...], jnp.where(eq, 1.0, 0.0).astype(tri_ref.dtype))
                keep = (blk > thr) | (eq & (rank + tie_seen <= need))
                tie_seen = tie_seen + rank[tk - 1:tk, :]
            else:
                keep = blk >= thr
            bias = jnp.where(keep, 0.0, -jnp.inf)
            start = pl.multiple_of(kb * tk, tk)

            def scores(h):
                p = h // 2
                return _dot(k_ref[0, pl.ds(start, tk), p * LANES:(p + 1) * LANES], qm_ref[h]) + bias

            ms = _softmax_heads(N_HEADS_A, scores,
                                lambda h: vt_ref[0, kb, h * V_ROWS:(h + 1) * V_ROWS, :],
                                ms, acc_ref, s_ref)
            return tie_seen, ms

        return lax.fori_loop(0, nkb, body, (row(0.0), (row(NEG_BIG),) * N_HEADS_A))[0]

    lax.cond(no_cut, lambda: attend(False), lambda: attend(True))
    _finish_heads(o_ref, acc_ref, N_HEADS_A)


def _dsa_attention(qt, k, vt, qit, ki, wt, topk, tq, tk):
    b, s, _ = k.shape
    nq, nk = s // tq, s // tk
    nqi = IDX_HEADS * IDX_DIM
    tri = jnp.asarray(np.tril(np.ones((tk, tk), np.float32))).astype(MXU_DTYPE)
    feat = lambda n: pl.BlockSpec((n, tq), lambda bi, i: (0, bi * nq + i))
    full = lambda n: pl.BlockSpec((1, s, n), lambda bi, i: (bi, 0, 0))
    vrows = N_HEADS_A * V_ROWS
    return pl.pallas_call(
        functools.partial(_dsa_kernel, tq=tq, tk=tk, topk=topk),
        grid=(b, nq),
        in_specs=[feat(nqi), full(IDX_DIM), feat(IDX_HEADS), feat(HW), full(HW),
                  pl.BlockSpec((1, nk, vrows, tk), lambda bi, i: (bi, 0, 0, 0)),
                  pl.BlockSpec((tk, tk), lambda bi, i: (0, 0))],
        out_specs=pl.BlockSpec((1, tq, HW), lambda bi, i: (bi, i, 0)),
        out_shape=jax.ShapeDtypeStruct((b, s, HW), MXU_DTYPE),
        scratch_shapes=[pltpu.VMEM((nk, tk, tq), jnp.int32),
                        pltpu.VMEM((N_HEADS_A, LANES, tq), MXU_DTYPE),
                        pltpu.VMEM((N_HEADS_A, V_ROWS, tq), F32),
                        pltpu.VMEM((2, tk, tq), F32)],
        compiler_params=_cparams("arbitrary", "arbitrary"),
        name="dsa_attention",
    )(qit, ki, wt, qt, k, vt.reshape(b, nk, vrows, tk), tri)


def _forget_kernel(qt_ref, k_ref, vt_ref, fcol_ref, frow_ref, o_ref,
                   qm_ref, acc_ref, s_ref, *, tq, tk):
    i = pl.program_id(1)
    nkb = (i * tq + tq - 1) // tk + 1
    q_pos = i * tq + lax.broadcasted_iota(jnp.int32, (1, tq), 1)
    _store_masked_qt(qm_ref, qt_ref, N_HEADS_C)
    f_q = frow_ref[0, 0] * LOG2E
    acc_ref[...] = jnp.zeros(acc_ref.shape, F32)

    def step(kb, ms, last):
        start = pl.multiple_of(kb * tk, tk)
        f_k = fcol_ref[0, pl.ds(start, tk), :] * LOG2E
        if last:
            causal = kb * tk + lax.broadcasted_iota(jnp.int32, (tk, 1), 0) <= q_pos

        def scores(h):
            p = h // 2
            s = _dot(k_ref[0, pl.ds(start, tk), p * LANES:(p + 1) * LANES], qm_ref[h])
            s = s + (f_q[h:h + 1, :] - f_k[:, SM_F + h:SM_F + h + 1])
            return jnp.where(causal, s, -jnp.inf) if last else s

        return _softmax_heads(N_HEADS_C, scores,
                              lambda h: vt_ref[0, kb, h * V_ROWS:(h + 1) * V_ROWS, :],
                              ms, acc_ref, s_ref)

    ms = lax.fori_loop(0, nkb - 1, lambda kb, ms: step(kb, ms, False),
                       (jnp.full((1, tq), NEG_BIG, F32),) * N_HEADS_C)
    step(nkb - 1, ms, True)
    _finish_heads(o_ref, acc_ref, N_HEADS_C)


def _forget_attention(qt, k, vt, fcol, frow, tq, tk):
    b, s, _ = k.shape
    nq, nk = s // tq, s // tk
    vrows = N_HEADS_C * V_ROWS
    return pl.pallas_call(
        functools.partial(_forget_kernel, tq=tq, tk=tk),
        grid=(b, nq),
        in_specs=[pl.BlockSpec((HW, tq), lambda bi, i: (0, bi * nq + i)),
                  pl.BlockSpec((1, s, HW), lambda bi, i: (bi, 0, 0)),
                  pl.BlockSpec((1, nk, vrows, tk), lambda bi, i: (bi, 0, 0, 0)),
                  pl.BlockSpec((1, s, LANES), lambda bi, i: (bi, 0, 0)),
                  pl.BlockSpec((1, 1, SUBLANES, tq), lambda bi, i: (bi, i, 0, 0))],
        out_specs=pl.BlockSpec((1, tq, HW), lambda bi, i: (bi, i, 0)),
        out_shape=jax.ShapeDtypeStruct((b, s, HW), MXU_DTYPE),
        scratch_shapes=[pltpu.VMEM((N_HEADS_C, LANES, tq), MXU_DTYPE),
                        pltpu.VMEM((N_HEADS_C, V_ROWS, tq), F32),
                        pltpu.VMEM((2, tk, tq), F32)],
        compiler_params=_cparams("arbitrary", "arbitrary"),
        name="forget_attention",
    )(qt, k, vt.reshape(b, nk, vrows, tk), fcol, frow)


def _even_lanes(rows):
    return lax.broadcasted_iota(jnp.int32, (rows, LANES), 1) < HEAD_DIM


def _band_kernel(q_ref, k_ref, v_ref, bt_ref, o_ref, qm_ref, *, tq):
    i = pl.program_id(1)
    win = tq + BAND_PAD
    start = pl.multiple_of(i * tq, tq)
    even = _even_lanes(tq)
    q = q_ref[0]
    zero = jnp.zeros((tq, LANES), q.dtype)
    for h in range(N_HEADS_B):
        slab = q[:, (h // 2) * LANES:(h // 2 + 1) * LANES]
        qm_ref[h] = jnp.where(even, slab, zero) if h % 2 == 0 else jnp.where(even, zero, slab)
    k_win = k_ref[0, pl.ds(start, win), :]
    v_win = v_ref[0, pl.ds(start, win), :]
    col = lax.broadcasted_iota(jnp.int32, (tq, win), 1)
    in_seq = col >= BAND_PAD - i * tq
    for p in range(HW // LANES):
        kp = k_win[:, p * LANES:(p + 1) * LANES]
        vp = v_win[:, p * LANES:(p + 1) * LANES]
        outs = []
        for h in range(2 * p, min(2 * p + 2, N_HEADS_B)):
            s = jnp.where(in_seq, _dot_nt(qm_ref[h], kp) + bt_ref[h], NEG_BIG)
            pe = jnp.exp(s - jnp.max(s, axis=1, keepdims=True))
            outs.append(_dot(pe.astype(vp.dtype), vp) / jnp.sum(pe, axis=1, keepdims=True))
        o = jnp.where(even, outs[0], outs[1]) if len(outs) == 2 else outs[0]
        o_ref[0, :, p * LANES:(p + 1) * LANES] = o.astype(o_ref.dtype)


def _band_bias_kernel(rb_ref, o_ref, *, tq):
    h = pl.program_id(0)
    win = tq + BAND_PAD
    r = lax.broadcasted_iota(jnp.int32, (tq, win), 0)
    c = lax.broadcasted_iota(jnp.int32, (tq, win), 1)
    dist = jnp.clip(r + BAND_PAD - c, -REL_CLIP, REL_CLIP) + REL_CLIP
    back = r // CHUNK + PREV_CHUNKS - c // CHUNK
    table = lax.fori_loop(0, 2 * REL_CLIP + 1,
                          lambda j, acc: jnp.where(dist == j, rb_ref[h, j], acc),
                          jnp.zeros((tq, win), F32))
    o_ref[0] = jnp.where((back >= 0) & (back <= PREV_CHUNKS), table, NEG_BIG)


def _band_attention(qkv, rel_bias):
    b, s, _ = qkv.shape
    tq = min(128, s)
    win = tq + BAND_PAD
    kv = jnp.pad(qkv[:, :, HW:], ((0, 0), (BAND_PAD, 0), (0, 0)))
    bt = pl.pallas_call(
        functools.partial(_band_bias_kernel, tq=tq),
        grid=(N_HEADS_B,),
        in_specs=[pl.BlockSpec(memory_space=pltpu.SMEM)],
        out_specs=pl.BlockSpec((1, tq, win), lambda h: (h, 0, 0)),
        out_shape=jax.ShapeDtypeStruct((N_HEADS_B, tq, win), F32),
        compiler_params=_cparams("arbitrary"),
        name="band_bias",
    )(rel_bias)
    return pl.pallas_call(
        functools.partial(_band_kernel, tq=tq),
        grid=(b, s // tq),
        in_specs=[pl.BlockSpec((1, tq, HW), lambda bi, i: (bi, i, 0)),
                  pl.BlockSpec((1, s + BAND_PAD, HW), lambda bi, i: (bi, 0, 0)),
                  pl.BlockSpec((1, s + BAND_PAD, HW), lambda bi, i: (bi, 0, 1)),
                  pl.BlockSpec((N_HEADS_B, tq, win), lambda bi, i: (0, 0, 0))],
        out_specs=pl.BlockSpec((1, tq, HW), lambda bi, i: (bi, i, 0)),
        out_shape=jax.ShapeDtypeStruct((b, s, HW), MXU_DTYPE),
        scratch_shapes=[pltpu.VMEM((N_HEADS_B, tq, LANES), MXU_DTYPE)],
        compiler_params=_cparams("arbitrary", "arbitrary"),
        name="band_attention",
    )(qkv, kv, kv, bt)


def _merge_kernel(x_ref, h_ref, oa_ref, ob_ref, oc_ref, wg_ref, bg_ref, wb_ref, wo_ref, g_ref, o_ref):
    h = h_ref[0]
    d = h.shape[1]
    y = None
    for br, o_br in enumerate((oa_ref, ob_ref, oc_ref)):
        gate = _sigmoid(_dot(h, wg_ref[:, br * d:(br + 1) * d]) + bg_ref[:, br * d:(br + 1) * d])
        term = gate * _dot(o_br[0], wb_ref[br])
        y = term if y is None else y + term
    o_ref[0] = x_ref[0] + g_ref[0] * _dot(y.astype(wo_ref.dtype), wo_ref[...])


def _merge(x, h, o_a, o_b, o_c, gates_w, gates_b, wb, w_out, mod, g_blk):
    b, s, d = x.shape
    tm = min(s, 512)
    tok = lambda n: pl.BlockSpec((1, tm, n), lambda bi, i: (bi, i, 0))
    const2 = lambda shape: pl.BlockSpec(shape, lambda bi, i: (0, 0))
    return pl.pallas_call(
        _merge_kernel,
        grid=(b, s // tm),
        in_specs=[tok(d), tok(d), tok(HW), tok(HW), tok(HW),
                  const2((d, N_BRANCH * d)), const2((1, N_BRANCH * d)),
                  pl.BlockSpec((N_BRANCH, HW, d), lambda bi, i: (0, 0, 0)),
                  const2((d, d)),
                  pl.BlockSpec((1, 1, d), lambda bi, i: (bi, 0, g_blk))],
        out_specs=tok(d),
        out_shape=jax.ShapeDtypeStruct((b, s, d), F32),
        compiler_params=_cparams("arbitrary", "arbitrary"),
        name="gated_merge",
    )(x, h, o_a, o_b, o_c, gates_w, gates_b, wb, w_out, mod)


def _ffn_kernel(x_ref, h_ref, wg_ref, wu_ref, wo_ref, g_ref, o_ref, acc_ref):
    j = pl.program_id(2)

    @pl.when(j == 0)
    def _():
        acc_ref[...] = jnp.zeros_like(acc_ref)

    h = h_ref[0]
    g = _dot(h, wg_ref[...])
    u = _dot(h, wu_ref[...])
    act = (g * _sigmoid(g)) * u
    acc_ref[...] += _dot(act.astype(wo_ref.dtype), wo_ref[...])

    @pl.when(j == pl.num_programs(2) - 1)
    def _():
        o_ref[0] = x_ref[0] + g_ref[0] * acc_ref[...]


def _ffn(x, h, w_in, w_out, mod, g_blk):
    b, s, d = x.shape
    tm = min(s, 1024)
    tf = 256
    nf = FFN_HIDDEN // tf
    tok = lambda: pl.BlockSpec((1, tm, d), lambda bi, i, j: (bi, i, 0))
    return pl.pallas_call(
        _ffn_kernel,
        grid=(b, s // tm, nf),
        in_specs=[tok(), tok(),
                  pl.BlockSpec((d, tf), lambda bi, i, j: (0, j)),
                  pl.BlockSpec((d, tf), lambda bi, i, j: (0, nf + j)),
                  pl.BlockSpec((tf, d), lambda bi, i, j: (j, 0)),
                  pl.BlockSpec((1, 1, d), lambda bi, i, j: (bi, 0, g_blk))],
        out_specs=tok(),
        out_shape=jax.ShapeDtypeStruct((b, s, d), F32),
        scratch_shapes=[pltpu.VMEM((tm, d), F32)],
        compiler_params=_cparams("arbitrary", "arbitrary", "arbitrary"),
        name="swiglu_ffn",
    )(x, h, w_in, w_in, w_out, mod)


def _pack_branch_weights(w_branch):
    pad = lambda a: jnp.pad(a, ((0, HW - a.shape[0]), (0, 0)))
    return jnp.stack([pad(w_branch[:WIDTH_A]),
                      pad(w_branch[WIDTH_A:WIDTH_A + WIDTH_B]),
                      pad(w_branch[WIDTH_A + WIDTH_B:])]).astype(MXU_DTYPE)


def kernel(x, c, positions, ada_w, ada_b, norm1_g, w_in, b_in, rel_bias, w_branch, w_out,
           norm2_g, w_ffn_in, w_ffn_out, final_g):
    b, s, d = x.shape
    depth = ada_w.shape[0]
    topk = min(TOPK_MAX, s // 4)
    tq, tk = min(Q_BLOCK, s), min(K_BLOCK, s)
    cos, sa, sb = _rope_tables(positions)
    mod_all = _ada_mod(c, ada_w, ada_b)
    for l in range(depth):
        mod = mod_all[l].reshape(b, 1, 6 * d)
        w_all, b_all, s_all, gates_w, gates_b = _pack_in_weights(w_in[l], b_in[l])
        h = _norm(x, norm1_g[l], mod, 1, 0, MXU_DTYPE)
        (qt_a, k_a, vt_a, qit, ki, wt, sm, qkv_b, qt_c, k_c, vt_c) = _proj(
            h.reshape(b * s, d), w_all, b_all, s_all, cos, sa, sb, tk)
        k_a, ki, sm, qkv_b, k_c = (a.reshape(b, s, a.shape[-1]) for a in (k_a, ki, sm, qkv_b, k_c))
        o_a = _dsa_attention(qt_a, k_a, vt_a, qit, ki, wt, topk, tq, tk)
        o_b = _band_attention(qkv_b, rel_bias[l])
        fcol, frow = _forget_cumsum(sm, tq)
        o_c = _forget_attention(qt_c, k_c, vt_c, fcol, frow, tq, tk)
        x = _merge(x, h, o_a, o_b, o_c, gates_w, gates_b, _pack_branch_weights(w_branch[l]),
                   w_out[l].astype(MXU_DTYPE), mod, 2)
        h = _norm(x, norm2_g[l], mod, 4, 3, MXU_DTYPE)
        x = _ffn(x, h, w_ffn_in[l].astype(MXU_DTYPE), w_ffn_out[l].astype(MXU_DTYPE), mod, 5)
    return _norm(x, final_g, None, 0, 0, x.dtype)
```

```python
import functools
import math

import numpy as np
import jax
import jax.numpy as jnp
from jax import lax
from jax.experimental import pallas as pl
from jax.experimental.pallas import tpu as pltpu

D_MODEL = 1024
CHUNK = 64
HEAD_DIM = 64
ROT_DIM = HEAD_DIM // 4
ROPE_THETA = 500000.0
N_HEADS_A = 6
N_HEADS_B = 5
N_HEADS_C = 5
WIDTH_A = N_HEADS_A * HEAD_DIM
WIDTH_B = N_HEADS_B * HEAD_DIM
WIDTH_C = N_HEADS_C * HEAD_DIM
IDX_HEADS = 8
IDX_DIM = 64
TOPK_MAX = 256
PREV_CHUNKS = 8
REL_CLIP = 128
N_BRANCH = 3
FFN_HIDDEN = -(-8 * D_MODEL // (3 * 256)) * 256
RMS_EPS = 1e-6
SPLIT_SIZES = (WIDTH_A, WIDTH_A, WIDTH_A, IDX_HEADS * IDX_DIM, IDX_DIM, IDX_HEADS,
               3 * WIDTH_B, 3 * WIDTH_C, N_HEADS_C, N_BRANCH * D_MODEL)

LANES = 128
SUBLANES = 8
PACKED_ROWS = 16
HW = 384
BAND_PAD = PREV_CHUNKS * CHUNK
VMEM_LIMIT = 56 * 1024 * 1024

MXU_DTYPE = jnp.bfloat16
F32 = jnp.float32
NEG_BIG = -1e30
INT_MIN = -2 ** 31
LOG2E = math.log2(math.e)

OFF_QK = 0
OFF_QI = 2 * HW
OFF_KI = OFF_QI + IDX_HEADS * IDX_DIM
OFF_SM = OFF_KI + LANES
OFF_VA = OFF_SM + LANES
OFF_B = OFF_VA + HW
OFF_C = OFF_B + 3 * HW
W_ALL = OFF_C + 3 * HW
SM_W = 0
SM_F = 8
Q_BLOCK = 256
K_BLOCK = 512
V_ROWS = HEAD_DIM + PACKED_ROWS
COUNT_ROWS = 4 * SUBLANES
SCORE_SLOTS = 6


def _cparams(*sem):
    return pltpu.CompilerParams(dimension_semantics=sem, vmem_limit_bytes=VMEM_LIMIT)


def _dot(a, b):
    return jnp.dot(a, b, preferred_element_type=F32)


def _dot_nt(a, b):
    return lax.dot_general(a, b, (((1,), (1,)), ((), ())), preferred_element_type=F32)


def _sigmoid(x):
    return 1.0 / (1.0 + jnp.exp(-x))


def _rope_kernel(pos_ref, invf_ref, ma_ref, mb_ref, cos_ref, sa_ref, sb_ref):
    ang = pos_ref[...].astype(F32) * invf_ref[...]
    s = jnp.sin(ang)
    cos_ref[...] = jnp.cos(ang)
    sa_ref[...] = s * ma_ref[...]
    sb_ref[...] = s * mb_ref[...]


def _rope_tables(positions):
    t = positions.size
    tm = min(t, 1024)
    half = ROT_DIM // 2
    d = np.arange(LANES) % HEAD_DIM
    inv = (ROPE_THETA ** (-(2.0 * (d % half)) / ROT_DIM)).astype(np.float32)
    invf = np.where(d < ROT_DIM, inv, 0.0).astype(np.float32)[None, :]
    ma = np.where(d < half, -1.0, 0.0).astype(np.float32)[None, :]
    mb = np.where((d >= half) & (d < ROT_DIM), 1.0, 0.0).astype(np.float32)[None, :]
    row = pl.BlockSpec((1, LANES), lambda i: (0, 0))
    tab = pl.BlockSpec((tm, LANES), lambda i: (i, 0))
    return pl.pallas_call(
        _rope_kernel,
        grid=(t // tm,),
        in_specs=[pl.BlockSpec((tm, 1), lambda i: (i, 0)), row, row, row],
        out_specs=[tab, tab, tab],
        out_shape=[jax.ShapeDtypeStruct((t, LANES), F32)] * 3,
        compiler_params=_cparams("arbitrary"),
        name="rope_tables",
    )(positions.reshape(t, 1), jnp.asarray(invf), jnp.asarray(ma), jnp.asarray(mb))


def _ada_kernel(c_ref, w_ref, b_ref, o_ref):
    c = c_ref[...]
    cond = c * _sigmoid(c)
    o_ref[0] = jnp.dot(cond, w_ref[0], precision=lax.Precision.HIGHEST,
                       preferred_element_type=F32) + b_ref[0]


def _ada_mod(c, ada_w, ada_b):
    depth, d, n = ada_w.shape
    b = c.shape[0]
    rows = -(-b // SUBLANES) * SUBLANES
    cp = jnp.pad(c, ((0, rows - b), (0, 0)))
    tn = 1024
    out = pl.pallas_call(
        _ada_kernel,
        grid=(depth, n // tn),
        in_specs=[pl.BlockSpec((rows, d), lambda l, j: (0, 0)),
                  pl.BlockSpec((1, d, tn), lambda l, j: (l, 0, j)),
                  pl.BlockSpec((1, 1, tn), lambda l, j: (l, 0, j))],
        out_specs=pl.BlockSpec((1, rows, tn), lambda l, j: (l, 0, j)),
        out_shape=jax.ShapeDtypeStruct((depth, rows, n), F32),
        compiler_params=_cparams("arbitrary", "arbitrary"),
        name="ada_mod",
    )(cp, ada_w, ada_b.reshape(depth, 1, n))
    return out[:, :b]


def _norm_kernel(x_ref, g_ref, *rest, modulate):
    x = x_ref[0]
    y = x * lax.rsqrt(jnp.mean(x * x, axis=-1, keepdims=True) + RMS_EPS) * g_ref[...]
    if modulate:
        sc_ref, sh_ref, o_ref = rest
        y = y * (1.0 + sc_ref[0]) + sh_ref[0]
    else:
        (o_ref,) = rest
    o_ref[0] = y.astype(o_ref.dtype)


def _norm(x, g, mod, sc_blk, sh_blk, out_dtype):
    b, s, d = x.shape
    tm = min(s, 512)
    xspec = pl.BlockSpec((1, tm, d), lambda bi, i: (bi, i, 0))
    in_specs = [xspec, pl.BlockSpec((1, d), lambda bi, i: (0, 0))]
    args = [x, g.reshape(1, d)]
    if mod is not None:
        in_specs += [pl.BlockSpec((1, 1, d), lambda bi, i: (bi, 0, sc_blk)),
                     pl.BlockSpec((1, 1, d), lambda bi, i: (bi, 0, sh_blk))]
        args += [mod, mod]
    return pl.pallas_call(
        functools.partial(_norm_kernel, modulate=mod is not None),
        grid=(b, s // tm),
        in_specs=in_specs,
        out_specs=xspec,
        out_shape=jax.ShapeDtypeStruct((b, s, d), out_dtype),
        compiler_params=_cparams("arbitrary", "arbitrary"),
        name="norm_mod" if mod is not None else "norm_final",
    )(*args)


def _proj_kernel(h_ref, w_ref, b_ref, s_ref, cos_ref, sa_ref, sb_ref,
                 qta_ref, ka_ref, vta_ref, qit_ref, ki_ref, wt_ref, sm_ref,
                 b_out_ref, qtc_ref, kc_ref, vtc_ref, *, tk):
    h = h_ref[...]
    tm = h.shape[0]

    def seg(a, n):
        return (_dot(h, w_ref[:, a:a + n]) + b_ref[:, a:a + n]) * s_ref[:, a:a + n]

    def rope(a, n):
        r = seg(a, n)
        cos, sa, sb = cos_ref[...], sa_ref[...], sb_ref[...]
        out = []
        for c in range(n // LANES):
            x = r[:, c * LANES:(c + 1) * LANES]
            out.append(x * cos + pltpu.roll(x, LANES - ROT_DIM // 2, 1) * sa
                       + pltpu.roll(x, ROT_DIM // 2, 1) * sb)
        return jnp.concatenate(out, axis=1)

    def store_vt(vt_ref, v, n_heads):
        ones = jnp.ones((V_ROWS - HEAD_DIM, tk), F32)
        for j in range(tm // tk):
            vt = v[j * tk:(j + 1) * tk, :].T
            rows = []
            for hd in range(n_heads):
                rows += [vt[hd * HEAD_DIM:(hd + 1) * HEAD_DIM, :], ones]
            vt_ref[j] = jnp.concatenate(rows, axis=0).astype(vt_ref.dtype)

    qta_ref[...] = rope(OFF_QK, HW).T.astype(qta_ref.dtype)
    ka_ref[...] = rope(OFF_QK + HW, HW).astype(ka_ref.dtype)
    qit_ref[...] = rope(OFF_QI, IDX_HEADS * IDX_DIM).T.astype(qit_ref.dtype)
    ki_ref[...] = rope(OFF_KI, LANES)[:, :IDX_DIM].astype(ki_ref.dtype)
    sm = seg(OFF_SM, LANES)
    sm_ref[...] = sm
    wt_ref[...] = sm.T[SM_W:SM_W + IDX_HEADS, :]
    store_vt(vta_ref, seg(OFF_VA, HW), N_HEADS_A)
    for c in range(3):
        b_out_ref[:, c * HW:(c + 1) * HW] = seg(OFF_B + c * HW, HW).astype(b_out_ref.dtype)
    qtc_ref[...] = seg(OFF_C, HW).T.astype(qtc_ref.dtype)
    kc_ref[...] = seg(OFF_C + HW, HW).astype(kc_ref.dtype)
    store_vt(vtc_ref, seg(OFF_C + 2 * HW, HW), N_HEADS_C)


def _proj(h2d, w_all, b_all, s_all, cos, sa, sb, tk):
    t, d = h2d.shape
    tm = max(min(t, 512), tk)
    nqi = IDX_HEADS * IDX_DIM
    const = lambda shape: pl.BlockSpec(shape, lambda i: (0, 0))
    tok = lambda n: pl.BlockSpec((tm, n), lambda i: (i, 0))
    feat = lambda n: pl.BlockSpec((n, tm), lambda i: (0, i))
    vts = lambda nh: pl.BlockSpec((tm // tk, nh * V_ROWS, tk), lambda i: (i, 0, 0))
    act = lambda shape: jax.ShapeDtypeStruct(shape, MXU_DTYPE)
    return pl.pallas_call(
        functools.partial(_proj_kernel, tk=tk),
        grid=(t // tm,),
        in_specs=[tok(d), const((d, W_ALL)), const((1, W_ALL)), const((1, W_ALL)),
                  tok(LANES), tok(LANES), tok(LANES)],
        out_specs=[feat(HW), tok(HW), vts(N_HEADS_A), feat(nqi), tok(IDX_DIM), feat(IDX_HEADS),
                   tok(LANES), tok(3 * HW), feat(HW), tok(HW), vts(N_HEADS_C)],
        out_shape=[act((HW, t)), act((t, HW)), act((t // tk, N_HEADS_A * V_ROWS, tk)),
                   act((nqi, t)), act((t, IDX_DIM)),
                   jax.ShapeDtypeStruct((IDX_HEADS, t), F32),
                   jax.ShapeDtypeStruct((t, LANES), F32),
                   act((t, 3 * HW)), act((HW, t)), act((t, HW)),
                   act((t // tk, N_HEADS_C * V_ROWS, tk))],
        compiler_params=_cparams("arbitrary"),
        name="in_proj",
    )(h2d, w_all, b_all, s_all, cos, sa, sb)


_SRC = np.cumsum((0,) + SPLIT_SIZES)
_IN_SEGMENTS = (
    (_SRC[0], WIDTH_A, OFF_QK), (_SRC[1], WIDTH_A, OFF_QK + HW), (_SRC[2], WIDTH_A, OFF_VA),
    (_SRC[3], IDX_HEADS * IDX_DIM, OFF_QI), (_SRC[4], IDX_DIM, OFF_KI),
    (_SRC[5], IDX_HEADS, OFF_SM + SM_W), (_SRC[8], N_HEADS_C, OFF_SM + SM_F),
) + tuple((_SRC[6] + j * WIDTH_B, WIDTH_B, OFF_B + j * HW) for j in range(3)) \
  + tuple((_SRC[7] + j * WIDTH_C, WIDTH_C, OFF_C + j * HW) for j in range(3))


def _repack_kernel(w_ref, all_ref, gates_ref):
    all_ref[...] = jnp.zeros(all_ref.shape, all_ref.dtype)
    for src, width, dst in _IN_SEGMENTS:
        all_ref[:, dst:dst + width] = w_ref[0, :, int(src):int(src) + width].astype(all_ref.dtype)
    g0 = int(_SRC[9])
    gates_ref[...] = w_ref[0, :, g0:g0 + gates_ref.shape[1]].astype(gates_ref.dtype)


def _repack(a, layer, dtype):
    _, rows, n = a.shape
    tr = min(rows, LANES)
    ng = N_BRANCH * D_MODEL
    return pl.pallas_call(
        _repack_kernel,
        grid=(rows // tr,),
        in_specs=[pl.BlockSpec((1, tr, n), lambda i: (layer, i, 0))],
        out_specs=[pl.BlockSpec((tr, W_ALL), lambda i: (i, 0)), pl.BlockSpec((tr, ng), lambda i: (i, 0))],
        out_shape=[jax.ShapeDtypeStruct((rows, W_ALL), dtype), jax.ShapeDtypeStruct((rows, ng), dtype)],
        compiler_params=_cparams("arbitrary"),
        name="repack_in",
    )(a)


def _pack_in_weights(w_in, b_in, layer):
    w_all, gates_w = _repack(w_in, layer, MXU_DTYPE)
    b_all, gates_b = _repack(b_in[:, None, :], layer, F32)
    scale = np.ones((1, W_ALL), np.float32)
    qs = HEAD_DIM ** -0.5
    scale[:, OFF_QK:OFF_QK + HW] = qs * LOG2E
    scale[:, OFF_SM + SM_W:OFF_SM + SM_W + IDX_HEADS] = IDX_HEADS ** -0.5 * IDX_DIM ** -0.5
    scale[:, OFF_B:OFF_B + HW] = qs
    scale[:, OFF_C:OFF_C + HW] = qs * LOG2E
    return w_all, b_all, jnp.asarray(scale), gates_w, gates_b


def _cumsum_kernel(sm_ref, tri_ref, col_ref, row_ref, carry_ref):
    @pl.when(pl.program_id(1) == 0)
    def _():
        carry_ref[...] = jnp.zeros_like(carry_ref)

    x = sm_ref[0]
    lf = jnp.minimum(x, 0.0) - jnp.log(1.0 + jnp.exp(-jnp.abs(x)))
    cs = jnp.dot(tri_ref[...], lf, precision=lax.Precision.HIGHEST,
                 preferred_element_type=F32) + carry_ref[0:1, :]
    col_ref[0] = cs
    row_ref[0, 0] = cs.T[SM_F:SM_F + SUBLANES, :]
    carry_ref[...] = jnp.broadcast_to(cs[-1:, :], carry_ref.shape)


def _forget_cumsum(sm, tq):
    b, s, _ = sm.shape
    tri = jnp.asarray(np.tril(np.ones((tq, tq), np.float32)))
    return pl.pallas_call(
        _cumsum_kernel,
        grid=(b, s // tq),
        in_specs=[pl.BlockSpec((1, tq, LANES), lambda bi, i: (bi, i, 0)),
                  pl.BlockSpec((tq, tq), lambda bi, i: (0, 0))],
        out_specs=[pl.BlockSpec((1, tq, LANES), lambda bi, i: (bi, i, 0)),
                   pl.BlockSpec((1, 1, SUBLANES, tq), lambda bi, i: (bi, i, 0, 0))],
        out_shape=[jax.ShapeDtypeStruct((b, s, LANES), F32),
                   jax.ShapeDtypeStruct((b, s // tq, SUBLANES, tq), F32)],
        scratch_shapes=[pltpu.VMEM((SUBLANES, LANES), F32)],
        compiler_params=_cparams("arbitrary", "arbitrary"),
        name="forget_cumsum",
    )(sm, tri)


def _store_masked_qt(qm_ref, qt_ref, n_heads):
    zero = jnp.zeros((HEAD_DIM, qm_ref.shape[2]), qm_ref.dtype)
    for h in range(n_heads):
        qh = qt_ref[h * HEAD_DIM:(h + 1) * HEAD_DIM, :]
        qm_ref[h] = jnp.concatenate([qh, zero] if h % 2 == 0 else [zero, qh], axis=0)


def _softmax_heads(n_heads, scores_fn, vt_fn, ms, acc_ref, s_ref):
    new_ms, alphas, pvs = [], [], []
    slots = s_ref.shape[0]
    for h in range(min(slots - 1, n_heads)):
        s_ref[h] = scores_fn(h)
    for h in range(n_heads):
        if h + slots - 1 < n_heads:
            s_ref[(h + slots - 1) % slots] = scores_fn(h + slots - 1)
        s = s_ref[h % slots]
        m_new = jnp.maximum(ms[h], jnp.max(s, axis=0, keepdims=True))
        alphas.append(jnp.exp2(ms[h] - m_new))
        pe = jnp.exp2(s - m_new)
        new_ms.append(m_new)
        vt = vt_fn(h)
        pvs.append(_dot(vt, pe.astype(vt.dtype)))
        if h > 0:
            acc_ref[h - 1] = acc_ref[h - 1] * alphas[h - 1] + pvs[h - 1]
    h = n_heads - 1
    acc_ref[h] = acc_ref[h] * alphas[h] + pvs[h]
    return tuple(new_ms)


def _finish_heads(o_ref, acc_ref, n_heads):
    outs = [acc_ref[h, :HEAD_DIM, :] / acc_ref[h, HEAD_DIM:HEAD_DIM + 1, :] for h in range(n_heads)]
    pad = HW // HEAD_DIM - n_heads
    if pad:
        outs.append(jnp.zeros((pad * HEAD_DIM, acc_ref.shape[2]), F32))
    o_ref[0] = jnp.concatenate(outs, axis=0).T.astype(o_ref.dtype)


def _dsa_kernel(qit_ref, ki_ref, wt_ref, qt_ref, k_ref, vt_ref, tri_ref, o_ref,
                keys_ref, top_ref, qm_ref, acc_ref, s_ref, *, tq, tk, topk):
    i = pl.program_id(1)
    nkb = (i * tq + tq - 1) // tk + 1
    q_pos = i * tq + lax.broadcasted_iota(jnp.int32, (1, tq), 1)
    q_chunk = q_pos // CHUNK
    _store_masked_qt(qm_ref, qt_ref, N_HEADS_A)

    def store_scores(kb, last):
        k_blk = ki_ref[0, pl.ds(pl.multiple_of(kb * tk, tk), tk), :]
        acc = jnp.zeros((tk, tq), F32)
        for h in range(IDX_HEADS):
            s = _dot(k_blk, qit_ref[h * IDX_DIM:(h + 1) * IDX_DIM, :])
            acc = acc + wt_ref[h:h + 1, :] * jnp.maximum(s, 0.0)
        if last:
            k_chunk = (kb * tk + lax.broadcasted_iota(jnp.int32, (tk, 1), 0)) // CHUNK
            acc = jnp.where(k_chunk <= q_chunk, acc, -jnp.inf)
        bits = lax.bitcast_convert_type(acc, jnp.int32)
        keys = bits ^ ((bits >> 31) & jnp.int32(0x7FFFFFFF))
        keys_ref[kb] = keys
        top_ref[kb] = (keys >> 16).astype(top_ref.dtype)

    def score_body(kb, carry):
        store_scores(kb, False)
        return carry

    lax.fori_loop(0, nkb - 1, score_body, 0)
    store_scores(nkb - 1, True)

    kcap = jnp.minimum(topk, (q_chunk + 1) * CHUNK)

    def count(pred, ref=keys_ref):
        dtype = ref.dtype
        one, zero = jnp.ones((), dtype), jnp.zeros((), dtype)

        def body(kb, acc):
            for c in range(tk // LANES):
                hit = jnp.where(pred(ref[kb, c * LANES:(c + 1) * LANES, :]), one, zero)
                parts = [hit[r:r + COUNT_ROWS] for r in range(0, LANES, COUNT_ROWS)]
                acc = acc + ((parts[0] + parts[1]) + (parts[2] + parts[3]))
            return acc

        acc = lax.fori_loop(0, nkb, body, jnp.zeros((COUNT_ROWS, tq), dtype))
        return jnp.sum(acc.astype(jnp.int32), axis=0, keepdims=True)

    def coarse_body(t, prefix):
        cand_u = prefix | (jnp.int32(1) << (15 - t))
        cand = (cand_u - jnp.int32(1 << 15)).astype(top_ref.dtype)
        return jnp.where(count(lambda blk: blk >= cand, top_ref) >= kcap, cand_u, prefix)

    top = lax.fori_loop(0, 16, coarse_body, jnp.zeros((1, tq), jnp.int32)) - jnp.int32(1 << 15)
    top16 = top.astype(top_ref.dtype)
    n_above = count(lambda blk: blk > top16, top_ref)

    lowest = jnp.full((), -(1 << 15), top_ref.dtype)

    def narrow_body(kb, carry):
        for c in range(tk // LANES):
            rows = slice(c * LANES, (c + 1) * LANES)
            low = (keys_ref[kb, rows, :] & jnp.int32(0xFFFF)) - jnp.int32(1 << 15)
            top_ref[kb, rows, :] = jnp.where(top_ref[kb, rows, :] == top16, low.astype(top_ref.dtype), lowest)
        return carry

    lax.fori_loop(0, nkb, narrow_body, 0)

    def fine_body(t, prefix):
        cand_u = prefix | (jnp.int32(1) << (15 - t))
        cand = (cand_u - jnp.int32(1 << 15)).astype(top_ref.dtype)
        n_ge = n_above + count(lambda blk: blk >= cand, top_ref)
        return jnp.where(n_ge >= kcap, cand_u, prefix)

    thr = (top << 16) | lax.fori_loop(0, 16, fine_body, jnp.zeros((1, tq), jnp.int32))
    n_gt = count(lambda blk: blk > thr)
    n_eq = count(lambda blk: blk >= thr) - n_gt
    no_cut = jnp.min(jnp.where(kcap - n_gt == n_eq, 1, 0)) == 1
    need = (kcap - n_gt).astype(F32)

    acc_ref[...] = jnp.zeros(acc_ref.shape, F32)
    row = lambda v: jnp.full((1, tq), v, F32)

    def attend(ranked):
        def body(kb, carry):
            tie_seen, ms = carry
            blk = keys_ref[kb]
            if ranked:
                eq = blk == thr
                rank = _dot(tri_ref[...], jnp.where(eq, 1.0, 0.0).astype(tri_ref.dtype))
                keep = (blk > thr) | (eq & (rank + tie_seen <= need))
                tie_seen = tie_seen + rank[tk - 1:tk, :]
            else:
                keep = blk >= thr
            bias = jnp.where(keep, 0.0, -jnp.inf)
            start = pl.multiple_of(kb * tk, tk)

            def scores(h):
                p = h // 2
                return _dot(k_ref[0, pl.ds(start, tk), p * LANES:(p + 1) * LANES], qm_ref[h]) + bias

            ms = _softmax_heads(N_HEADS_A, scores,
                                lambda h: vt_ref[0, kb, h * V_ROWS:(h + 1) * V_ROWS, :],
                                ms, acc_ref, s_ref)
            return tie_seen, ms

        return lax.fori_loop(0, nkb, body, (row(0.0), (row(NEG_BIG),) * N_HEADS_A))[0]

    lax.cond(no_cut, lambda: attend(False), lambda: attend(True))
    _finish_heads(o_ref, acc_ref, N_HEADS_A)


def _dsa_attention(qt, k, vt, qit, ki, wt, topk, tq, tk):
    b, s, _ = k.shape
    nq, nk = s // tq, s // tk
    nqi = IDX_HEADS * IDX_DIM
    tri = jnp.asarray(np.tril(np.ones((tk, tk), np.float32))).astype(MXU_DTYPE)
    feat = lambda n: pl.BlockSpec((n, tq), lambda bi, i: (0, bi * nq + i))
    full = lambda n: pl.BlockSpec((1, s, n), lambda bi, i: (bi, 0, 0))
    vrows = N_HEADS_A * V_ROWS
    return pl.pallas_call(
        functools.partial(_dsa_kernel, tq=tq, tk=tk, topk=topk),
        grid=(b, nq),
        in_specs=[feat(nqi), full(IDX_DIM), feat(IDX_HEADS), feat(HW), full(HW),
                  pl.BlockSpec((1, nk, vrows, tk), lambda bi, i: (bi, 0, 0, 0)),
                  pl.BlockSpec((tk, tk), lambda bi, i: (0, 0))],
        out_specs=pl.BlockSpec((1, tq, HW), lambda bi, i: (bi, i, 0)),
        out_shape=jax.ShapeDtypeStruct((b, s, HW), MXU_DTYPE),
        scratch_shapes=[pltpu.VMEM((nk, tk, tq), jnp.int32),
                        pltpu.VMEM((nk, tk, tq), jnp.int16),
                        pltpu.VMEM((N_HEADS_A, LANES, tq), MXU_DTYPE),
                        pltpu.VMEM((N_HEADS_A, V_ROWS, tq), F32),
                        pltpu.VMEM((SCORE_SLOTS, tk, tq), F32)],
        compiler_params=_cparams("arbitrary", "arbitrary"),
        name="dsa_attention",
    )(qit, ki, wt, qt, k, vt.reshape(b, nk, vrows, tk), tri)


def _forget_kernel(qt_ref, k_ref, vt_ref, fcol_ref, frow_ref, o_ref,
                   qm_ref, acc_ref, s_ref, *, tq, tk):
    i = pl.program_id(1)
    nkb = (i * tq + tq - 1) // tk + 1
    q_pos = i * tq + lax.broadcasted_iota(jnp.int32, (1, tq), 1)
    _store_masked_qt(qm_ref, qt_ref, N_HEADS_C)
    f_q = frow_ref[0, 0] * LOG2E
    acc_ref[...] = jnp.zeros(acc_ref.shape, F32)

    def step(kb, ms, last):
        start = pl.multiple_of(kb * tk, tk)
        f_k = fcol_ref[0, pl.ds(start, tk), :] * LOG2E
        if last:
            causal = kb * tk + lax.broadcasted_iota(jnp.int32, (tk, 1), 0) <= q_pos

        def scores(h):
            p = h // 2
            s = _dot(k_ref[0, pl.ds(start, tk), p * LANES:(p + 1) * LANES], qm_ref[h])
            s = s + (f_q[h:h + 1, :] - f_k[:, SM_F + h:SM_F + h + 1])
            return jnp.where(causal, s, -jnp.inf) if last else s

        return _softmax_heads(N_HEADS_C, scores,
                              lambda h: vt_ref[0, kb, h * V_ROWS:(h + 1) * V_ROWS, :],
                              ms, acc_ref, s_ref)

    ms = lax.fori_loop(0, nkb - 1, lambda kb, ms: step(kb, ms, False),
                       (jnp.full((1, tq), NEG_BIG, F32),) * N_HEADS_C)
    step(nkb - 1, ms, True)
    _finish_heads(o_ref, acc_ref, N_HEADS_C)


def _forget_attention(qt, k, vt, fcol, frow, tq, tk):
    b, s, _ = k.shape
    nq, nk = s // tq, s // tk
    vrows = N_HEADS_C * V_ROWS
    return pl.pallas_call(
        functools.partial(_forget_kernel, tq=tq, tk=tk),
        grid=(b, nq),
        in_specs=[pl.BlockSpec((HW, tq), lambda bi, i: (0, bi * nq + i)),
                  pl.BlockSpec((1, s, HW), lambda bi, i: (bi, 0, 0)),
                  pl.BlockSpec((1, nk, vrows, tk), lambda bi, i: (bi, 0, 0, 0)),
                  pl.BlockSpec((1, s, LANES), lambda bi, i: (bi, 0, 0)),
                  pl.BlockSpec((1, 1, SUBLANES, tq), lambda bi, i: (bi, i, 0, 0))],
        out_specs=pl.BlockSpec((1, tq, HW), lambda bi, i: (bi, i, 0)),
        out_shape=jax.ShapeDtypeStruct((b, s, HW), MXU_DTYPE),
        scratch_shapes=[pltpu.VMEM((N_HEADS_C, LANES, tq), MXU_DTYPE),
                        pltpu.VMEM((N_HEADS_C, V_ROWS, tq), F32),
                        pltpu.VMEM((SCORE_SLOTS, tk, tq), F32)],
        compiler_params=_cparams("arbitrary", "arbitrary"),
        name="forget_attention",
    )(qt, k, vt.reshape(b, nk, vrows, tk), fcol, frow)


def _even_lanes(rows):
    return lax.broadcasted_iota(jnp.int32, (rows, LANES), 1) < HEAD_DIM


def _band_kernel(q_ref, k_ref, v_ref, bt_ref, o_ref, qm_ref, *, tq):
    i = pl.program_id(1)
    win = tq + BAND_PAD
    start = pl.multiple_of(i * tq, tq)
    even = _even_lanes(tq)
    q = q_ref[0]
    zero = jnp.zeros((tq, LANES), q.dtype)
    for h in range(N_HEADS_B):
        slab = q[:, (h // 2) * LANES:(h // 2 + 1) * LANES]
        qm_ref[h] = jnp.where(even, slab, zero) if h % 2 == 0 else jnp.where(even, zero, slab)
    k_win = k_ref[0, pl.ds(start, win), :]
    v_win = v_ref[0, pl.ds(start, win), :]
    col = lax.broadcasted_iota(jnp.int32, (tq, win), 1)
    in_seq = col >= BAND_PAD - i * tq
    for p in range(HW // LANES):
        kp = k_win[:, p * LANES:(p + 1) * LANES]
        vp = v_win[:, p * LANES:(p + 1) * LANES]
        outs = []
        for h in range(2 * p, min(2 * p + 2, N_HEADS_B)):
            s = jnp.where(in_seq, _dot_nt(qm_ref[h], kp) + bt_ref[h], NEG_BIG)
            pe = jnp.exp(s - jnp.max(s, axis=1, keepdims=True))
            outs.append(_dot(pe.astype(vp.dtype), vp) / jnp.sum(pe, axis=1, keepdims=True))
        o = jnp.where(even, outs[0], outs[1]) if len(outs) == 2 else outs[0]
        o_ref[0, :, p * LANES:(p + 1) * LANES] = o.astype(o_ref.dtype)


def _band_bias_kernel(rb_ref, o_ref, *, tq):
    h = pl.program_id(0)
    win = tq + BAND_PAD
    r = lax.broadcasted_iota(jnp.int32, (tq, win), 0)
    c = lax.broadcasted_iota(jnp.int32, (tq, win), 1)
    dist = jnp.clip(r + BAND_PAD - c, -REL_CLIP, REL_CLIP) + REL_CLIP
    back = r // CHUNK + PREV_CHUNKS - c // CHUNK
    table = lax.fori_loop(0, 2 * REL_CLIP + 1,
                          lambda j, acc: jnp.where(dist == j, rb_ref[h, j], acc),
                          jnp.zeros((tq, win), F32))
    o_ref[0] = jnp.where((back >= 0) & (back <= PREV_CHUNKS), table, NEG_BIG)


def _band_attention(qkv, rel_bias):
    b, s, _ = qkv.shape
    tq = min(128, s)
    win = tq + BAND_PAD
    kv = jnp.pad(qkv[:, :, HW:], ((0, 0), (BAND_PAD, 0), (0, 0)))
    bt = pl.pallas_call(
        functools.partial(_band_bias_kernel, tq=tq),
        grid=(N_HEADS_B,),
        in_specs=[pl.BlockSpec(memory_space=pltpu.SMEM)],
        out_specs=pl.BlockSpec((1, tq, win), lambda h: (h, 0, 0)),
        out_shape=jax.ShapeDtypeStruct((N_HEADS_B, tq, win), F32),
        compiler_params=_cparams("arbitrary"),
        name="band_bias",
    )(rel_bias)
    return pl.pallas_call(
        functools.partial(_band_kernel, tq=tq),
        grid=(b, s // tq),
        in_specs=[pl.BlockSpec((1, tq, HW), lambda bi, i: (bi, i, 0)),
                  pl.BlockSpec((1, s + BAND_PAD, HW), lambda bi, i: (bi, 0, 0)),
                  pl.BlockSpec((1, s + BAND_PAD, HW), lambda bi, i: (bi, 0, 1)),
                  pl.BlockSpec((N_HEADS_B, tq, win), lambda bi, i: (0, 0, 0))],
        out_specs=pl.BlockSpec((1, tq, HW), lambda bi, i: (bi, i, 0)),
        out_shape=jax.ShapeDtypeStruct((b, s, HW), MXU_DTYPE),
        scratch_shapes=[pltpu.VMEM((N_HEADS_B, tq, LANES), MXU_DTYPE)],
        compiler_params=_cparams("arbitrary", "arbitrary"),
        name="band_attention",
    )(qkv, kv, kv, bt)


def _merge_kernel(x_ref, h_ref, oa_ref, ob_ref, oc_ref, wg_ref, bg_ref, wb_ref, wo_ref, g_ref, o_ref):
    h = h_ref[0]
    d = h.shape[1]
    y = None
    for br, o_br in enumerate((oa_ref, ob_ref, oc_ref)):
        gate = _sigmoid(_dot(h, wg_ref[:, br * d:(br + 1) * d]) + bg_ref[:, br * d:(br + 1) * d])
        term = gate * _dot(o_br[0], wb_ref[br])
        y = term if y is None else y + term
    o_ref[0] = x_ref[0] + g_ref[0] * _dot(y.astype(wo_ref.dtype), wo_ref[...])


def _merge(x, h, o_a, o_b, o_c, gates_w, gates_b, wb, w_out, mod, g_blk):
    b, s, d = x.shape
    tm = min(s, 512)
    tok = lambda n: pl.BlockSpec((1, tm, n), lambda bi, i: (bi, i, 0))
    const2 = lambda shape: pl.BlockSpec(shape, lambda bi, i: (0, 0))
    return pl.pallas_call(
        _merge_kernel,
        grid=(b, s // tm),
        in_specs=[tok(d), tok(d), tok(HW), tok(HW), tok(HW),
                  const2((d, N_BRANCH * d)), const2((1, N_BRANCH * d)),
                  pl.BlockSpec((N_BRANCH, HW, d), lambda bi, i: (0, 0, 0)),
                  const2((d, d)),
                  pl.BlockSpec((1, 1, d), lambda bi, i: (bi, 0, g_blk))],
        out_specs=tok(d),
        out_shape=jax.ShapeDtypeStruct((b, s, d), F32),
        compiler_params=_cparams("arbitrary", "arbitrary"),
        name="gated_merge",
    )(x, h, o_a, o_b, o_c, gates_w, gates_b, wb, w_out, mod)


def _ffn_kernel(x_ref, h_ref, wg_ref, wu_ref, wo_ref, g_ref, o_ref, acc_ref):
    j = pl.program_id(2)

    @pl.when(j == 0)
    def _():
        acc_ref[...] = jnp.zeros_like(acc_ref)

    h = h_ref[0]
    g = _dot(h, wg_ref[...])
    u = _dot(h, wu_ref[...])
    act = (g * _sigmoid(g)) * u
    acc_ref[...] += _dot(act.astype(wo_ref.dtype), wo_ref[...])

    @pl.when(j == pl.num_programs(2) - 1)
    def _():
        o_ref[0] = x_ref[0] + g_ref[0] * acc_ref[...]


def _ffn(x, h, w_in, w_out, mod, g_blk):
    b, s, d = x.shape
    tm = min(s, 1024)
    tf = 256
    nf = FFN_HIDDEN // tf
    tok = lambda: pl.BlockSpec((1, tm, d), lambda bi, i, j: (bi, i, 0))
    return pl.pallas_call(
        _ffn_kernel,
        grid=(b, s // tm, nf),
        in_specs=[tok(), tok(),
                  pl.BlockSpec((d, tf), lambda bi, i, j: (0, j)),
                  pl.BlockSpec((d, tf), lambda bi, i, j: (0, nf + j)),
                  pl.BlockSpec((tf, d), lambda bi, i, j: (j, 0)),
                  pl.BlockSpec((1, 1, d), lambda bi, i, j: (bi, 0, g_blk))],
        out_specs=tok(),
        out_shape=jax.ShapeDtypeStruct((b, s, d), F32),
        scratch_shapes=[pltpu.VMEM((tm, d), F32)],
        compiler_params=_cparams("arbitrary", "arbitrary", "arbitrary"),
        name="swiglu_ffn",
    )(x, h, w_in, w_in, w_out, mod)


def _pack_branch_weights(w_branch):
    pad = lambda a: jnp.pad(a, ((0, HW - a.shape[0]), (0, 0)))
    return jnp.stack([pad(w_branch[:WIDTH_A]),
                      pad(w_branch[WIDTH_A:WIDTH_A + WIDTH_B]),
                      pad(w_branch[WIDTH_A + WIDTH_B:])]).astype(MXU_DTYPE)


def kernel(x, c, positions, ada_w, ada_b, norm1_g, w_in, b_in, rel_bias, w_branch, w_out,
           norm2_g, w_ffn_in, w_ffn_out, final_g):
    b, s, d = x.shape
    depth = ada_w.shape[0]
    topk = min(TOPK_MAX, s // 4)
    tq, tk = min(Q_BLOCK, s), min(K_BLOCK, s)
    cos, sa, sb = _rope_tables(positions)
    mod_all = _ada_mod(c, ada_w, ada_b)
    for l in range(depth):
        mod = mod_all[l].reshape(b, 1, 6 * d)
        w_all, b_all, s_all, gates_w, gates_b = _pack_in_weights(w_in, b_in, l)
        h = _norm(x, norm1_g[l], mod, 1, 0, MXU_DTYPE)
        (qt_a, k_a, vt_a, qit, ki, wt, sm, qkv_b, qt_c, k_c, vt_c) = _proj(
            h.reshape(b * s, d), w_all, b_all, s_all, cos, sa, sb, tk)
        k_a, ki, sm, qkv_b, k_c = (a.reshape(b, s, a.shape[-1]) for a in (k_a, ki, sm, qkv_b, k_c))
        o_a = _dsa_attention(qt_a, k_a, vt_a, qit, ki, wt, topk, tq, tk)
        o_b = _band_attention(qkv_b, rel_bias[l])
        fcol, frow = _forget_cumsum(sm, tq)
        o_c = _forget_attention(qt_c, k_c, vt_c, fcol, frow, tq, tk)
        x = _merge(x, h, o_a, o_b, o_c, gates_w, gates_b, _pack_branch_weights(w_branch[l]),
                   w_out[l].astype(MXU_DTYPE), mod, 2)
        h = _norm(x, norm2_g[l], mod, 4, 3, MXU_DTYPE)
        x = _ffn(x, h, w_ffn_in[l].astype(MXU_DTYPE), w_ffn_out[l].astype(MXU_DTYPE), mod, 5)
    return _norm(x, final_g, None, 0, 0, x.dtype)
```

```python
import functools
import math

import numpy as np
import jax
import jax.numpy as jnp
from jax import lax
from jax.experimental import pallas as pl
from jax.experimental.pallas import tpu as pltpu

D_MODEL = 1024
CHUNK = 64
HEAD_DIM = 64
ROT_DIM = HEAD_DIM // 4
ROPE_THETA = 500000.0
N_HEADS_A = 6
N_HEADS_B = 5
N_HEADS_C = 5
WIDTH_A = N_HEADS_A * HEAD_DIM
WIDTH_B = N_HEADS_B * HEAD_DIM
WIDTH_C = N_HEADS_C * HEAD_DIM
IDX_HEADS = 8
IDX_DIM = 64
TOPK_MAX = 256
PREV_CHUNKS = 8
REL_CLIP = 128
N_BRANCH = 3
FFN_HIDDEN = -(-8 * D_MODEL // (3 * 256)) * 256
RMS_EPS = 1e-6
SPLIT_SIZES = (WIDTH_A, WIDTH_A, WIDTH_A, IDX_HEADS * IDX_DIM, IDX_DIM, IDX_HEADS,
               3 * WIDTH_B, 3 * WIDTH_C, N_HEADS_C, N_BRANCH * D_MODEL)

LANES = 128
SUBLANES = 8
PACKED_ROWS = 16
HW = 384
BAND_PAD = PREV_CHUNKS * CHUNK
VMEM_LIMIT = 56 * 1024 * 1024

MXU_DTYPE = jnp.bfloat16
F32 = jnp.float32
NEG_BIG = -1e30
INT_MIN = -2 ** 31
LOG2E = math.log2(math.e)

OFF_QK = 0
OFF_QI = 2 * HW
OFF_KI = OFF_QI + IDX_HEADS * IDX_DIM
OFF_SM = OFF_KI + LANES
OFF_VA = OFF_SM + LANES
OFF_B = OFF_VA + HW
OFF_C = OFF_B + 3 * HW
W_ALL = OFF_C + 3 * HW
SM_W = 0
SM_F = 8
Q_BLOCK = 512
K_BLOCK = 512
V_ROWS = HEAD_DIM + PACKED_ROWS
COUNT_ROWS = 4 * SUBLANES
BAND_KEYS = 128
SCORE_SLOTS = 6


def _cparams(*sem):
    return pltpu.CompilerParams(dimension_semantics=sem, vmem_limit_bytes=VMEM_LIMIT)


def _dot(a, b):
    return jnp.dot(a, b, preferred_element_type=F32)


def _dot_nt(a, b):
    return lax.dot_general(a, b, (((1,), (1,)), ((), ())), preferred_element_type=F32)


def _sigmoid(x):
    return 1.0 / (1.0 + jnp.exp(-x))


def _rope_kernel(pos_ref, invf_ref, ma_ref, mb_ref, cos_ref, sa_ref, sb_ref):
    ang = pos_ref[...].astype(F32) * invf_ref[...]
    s = jnp.sin(ang)
    cos_ref[...] = jnp.cos(ang)
    sa_ref[...] = s * ma_ref[...]
    sb_ref[...] = s * mb_ref[...]


def _rope_tables(positions):
    t = positions.size
    tm = min(t, 1024)
    half = ROT_DIM // 2
    d = np.arange(LANES) % HEAD_DIM
    inv = (ROPE_THETA ** (-(2.0 * (d % half)) / ROT_DIM)).astype(np.float32)
    invf = np.where(d < ROT_DIM, inv, 0.0).astype(np.float32)[None, :]
    ma = np.where(d < half, -1.0, 0.0).astype(np.float32)[None, :]
    mb = np.where((d >= half) & (d < ROT_DIM), 1.0, 0.0).astype(np.float32)[None, :]
    row = pl.BlockSpec((1, LANES), lambda i: (0, 0))
    tab = pl.BlockSpec((tm, LANES), lambda i: (i, 0))
    return pl.pallas_call(
        _rope_kernel,
        grid=(t // tm,),
        in_specs=[pl.BlockSpec((tm, 1), lambda i: (i, 0)), row, row, row],
        out_specs=[tab, tab, tab],
        out_shape=[jax.ShapeDtypeStruct((t, LANES), F32)] * 3,
        compiler_params=_cparams("arbitrary"),
        name="rope_tables",
    )(positions.reshape(t, 1), jnp.asarray(invf), jnp.asarray(ma), jnp.asarray(mb))


def _ada_kernel(c_ref, w_ref, b_ref, o_ref):
    c = c_ref[...]
    cond = c * _sigmoid(c)
    o_ref[0] = jnp.dot(cond, w_ref[0], precision=lax.Precision.HIGHEST,
                       preferred_element_type=F32) + b_ref[0]


def _ada_mod(c, ada_w, ada_b):
    depth, d, n = ada_w.shape
    b = c.shape[0]
    rows = -(-b // SUBLANES) * SUBLANES
    cp = jnp.pad(c, ((0, rows - b), (0, 0)))
    tn = 1024
    out = pl.pallas_call(
        _ada_kernel,
        grid=(depth, n // tn),
        in_specs=[pl.BlockSpec((rows, d), lambda l, j: (0, 0)),
                  pl.BlockSpec((1, d, tn), lambda l, j: (l, 0, j)),
                  pl.BlockSpec((1, 1, tn), lambda l, j: (l, 0, j))],
        out_specs=pl.BlockSpec((1, rows, tn), lambda l, j: (l, 0, j)),
        out_shape=jax.ShapeDtypeStruct((depth, rows, n), F32),
        compiler_params=_cparams("arbitrary", "arbitrary"),
        name="ada_mod",
    )(cp, ada_w, ada_b.reshape(depth, 1, n))
    return out[:, :b]


def _norm_kernel(x_ref, g_ref, *rest, modulate):
    x = x_ref[0]
    y = x * lax.rsqrt(jnp.mean(x * x, axis=-1, keepdims=True) + RMS_EPS) * g_ref[...]
    if modulate:
        sc_ref, sh_ref, o_ref = rest
        y = y * (1.0 + sc_ref[0]) + sh_ref[0]
    else:
        (o_ref,) = rest
    o_ref[0] = y.astype(o_ref.dtype)


def _norm(x, g, mod, sc_blk, sh_blk, out_dtype):
    b, s, d = x.shape
    tm = min(s, 512)
    xspec = pl.BlockSpec((1, tm, d), lambda bi, i: (bi, i, 0))
    in_specs = [xspec, pl.BlockSpec((1, d), lambda bi, i: (0, 0))]
    args = [x, g.reshape(1, d)]
    if mod is not None:
        in_specs += [pl.BlockSpec((1, 1, d), lambda bi, i: (bi, 0, sc_blk)),
                     pl.BlockSpec((1, 1, d), lambda bi, i: (bi, 0, sh_blk))]
        args += [mod, mod]
    return pl.pallas_call(
        functools.partial(_norm_kernel, modulate=mod is not None),
        grid=(b, s // tm),
        in_specs=in_specs,
        out_specs=xspec,
        out_shape=jax.ShapeDtypeStruct((b, s, d), out_dtype),
        compiler_params=_cparams("arbitrary", "arbitrary"),
        name="norm_mod" if mod is not None else "norm_final",
    )(*args)


def _proj_kernel(h_ref, w_ref, b_ref, s_ref, cos_ref, sa_ref, sb_ref,
                 qta_ref, ka_ref, vta_ref, qit_ref, ki_ref, wt_ref, sm_ref,
                 qtb_ref, kb_ref, vtb_ref, qtc_ref, kc_ref, vtc_ref, *, tk):
    h = h_ref[...]
    tm = h.shape[0]

    def seg(a, n):
        return (_dot(h, w_ref[:, a:a + n]) + b_ref[:, a:a + n]) * s_ref[:, a:a + n]

    def rope(a, n):
        r = seg(a, n)
        cos, sa, sb = cos_ref[...], sa_ref[...], sb_ref[...]
        out = []
        for c in range(n // LANES):
            x = r[:, c * LANES:(c + 1) * LANES]
            out.append(x * cos + pltpu.roll(x, LANES - ROT_DIM // 2, 1) * sa
                       + pltpu.roll(x, ROT_DIM // 2, 1) * sb)
        return jnp.concatenate(out, axis=1)

    def store_vt(vt_ref, v, n_heads):
        blk = vt_ref.shape[2]
        ones = jnp.ones((V_ROWS - HEAD_DIM, blk), F32)
        for j in range(tm // blk):
            vt = v[j * blk:(j + 1) * blk, :].T
            rows = []
            for hd in range(n_heads):
                rows += [vt[hd * HEAD_DIM:(hd + 1) * HEAD_DIM, :], ones]
            vt_ref[j] = jnp.concatenate(rows, axis=0).astype(vt_ref.dtype)

    qta_ref[...] = rope(OFF_QK, HW).T.astype(qta_ref.dtype)
    ka_ref[...] = rope(OFF_QK + HW, HW).astype(ka_ref.dtype)
    qit_ref[...] = rope(OFF_QI, IDX_HEADS * IDX_DIM).T.astype(qit_ref.dtype)
    ki_ref[...] = rope(OFF_KI, LANES)[:, :IDX_DIM].astype(ki_ref.dtype)
    sm = seg(OFF_SM, LANES)
    sm_ref[...] = sm
    wt_ref[...] = sm.T[SM_W:SM_W + IDX_HEADS, :]
    store_vt(vta_ref, seg(OFF_VA, HW), N_HEADS_A)
    qtb_ref[...] = seg(OFF_B, HW).T.astype(qtb_ref.dtype)
    kb_ref[...] = seg(OFF_B + HW, HW).astype(kb_ref.dtype)
    store_vt(vtb_ref, seg(OFF_B + 2 * HW, HW), N_HEADS_B)
    qtc_ref[...] = seg(OFF_C, HW).T.astype(qtc_ref.dtype)
    kc_ref[...] = seg(OFF_C + HW, HW).astype(kc_ref.dtype)
    store_vt(vtc_ref, seg(OFF_C + 2 * HW, HW), N_HEADS_C)


def _proj(h2d, w_all, b_all, s_all, cos, sa, sb, tk):
    t, d = h2d.shape
    tm = max(min(t, 512), tk)
    nqi = IDX_HEADS * IDX_DIM
    const = lambda shape: pl.BlockSpec(shape, lambda i: (0, 0))
    tok = lambda n: pl.BlockSpec((tm, n), lambda i: (i, 0))
    feat = lambda n: pl.BlockSpec((n, tm), lambda i: (0, i))
    vts = lambda nh, blk: pl.BlockSpec((tm // blk, nh * V_ROWS, blk), lambda i: (i, 0, 0))
    act = lambda shape: jax.ShapeDtypeStruct(shape, MXU_DTYPE)
    vshape = lambda nh, blk: act((t // blk, nh * V_ROWS, blk))
    bk = min(BAND_KEYS, tm)
    return pl.pallas_call(
        functools.partial(_proj_kernel, tk=tk),
        grid=(t // tm,),
        in_specs=[tok(d), const((d, W_ALL)), const((1, W_ALL)), const((1, W_ALL)),
                  tok(LANES), tok(LANES), tok(LANES)],
        out_specs=[feat(HW), tok(HW), vts(N_HEADS_A, tk), feat(nqi), tok(IDX_DIM), feat(IDX_HEADS),
                   tok(LANES), feat(HW), tok(HW), vts(N_HEADS_B, bk),
                   feat(HW), tok(HW), vts(N_HEADS_C, tk)],
        out_shape=[act((HW, t)), act((t, HW)), vshape(N_HEADS_A, tk),
                   act((nqi, t)), act((t, IDX_DIM)),
                   jax.ShapeDtypeStruct((IDX_HEADS, t), F32),
                   jax.ShapeDtypeStruct((t, LANES), F32),
                   act((HW, t)), act((t, HW)), vshape(N_HEADS_B, bk),
                   act((HW, t)), act((t, HW)), vshape(N_HEADS_C, tk)],
        compiler_params=_cparams("arbitrary"),
        name="in_proj",
    )(h2d, w_all, b_all, s_all, cos, sa, sb)


_SRC = np.cumsum((0,) + SPLIT_SIZES)
_IN_SEGMENTS = (
    (_SRC[0], WIDTH_A, OFF_QK), (_SRC[1], WIDTH_A, OFF_QK + HW), (_SRC[2], WIDTH_A, OFF_VA),
    (_SRC[3], IDX_HEADS * IDX_DIM, OFF_QI), (_SRC[4], IDX_DIM, OFF_KI),
    (_SRC[5], IDX_HEADS, OFF_SM + SM_W), (_SRC[8], N_HEADS_C, OFF_SM + SM_F),
) + tuple((_SRC[6] + j * WIDTH_B, WIDTH_B, OFF_B + j * HW) for j in range(3)) \
  + tuple((_SRC[7] + j * WIDTH_C, WIDTH_C, OFF_C + j * HW) for j in range(3))


def _repack_kernel(w_ref, all_ref, gates_ref):
    all_ref[...] = jnp.zeros(all_ref.shape, all_ref.dtype)
    for src, width, dst in _IN_SEGMENTS:
        all_ref[:, dst:dst + width] = w_ref[0, :, int(src):int(src) + width].astype(all_ref.dtype)
    g0 = int(_SRC[9])
    gates_ref[...] = w_ref[0, :, g0:g0 + gates_ref.shape[1]].astype(gates_ref.dtype)


def _repack(a, layer, dtype):
    _, rows, n = a.shape
    tr = min(rows, LANES)
    ng = N_BRANCH * D_MODEL
    return pl.pallas_call(
        _repack_kernel,
        grid=(rows // tr,),
        in_specs=[pl.BlockSpec((1, tr, n), lambda i: (layer, i, 0))],
        out_specs=[pl.BlockSpec((tr, W_ALL), lambda i: (i, 0)), pl.BlockSpec((tr, ng), lambda i: (i, 0))],
        out_shape=[jax.ShapeDtypeStruct((rows, W_ALL), dtype), jax.ShapeDtypeStruct((rows, ng), dtype)],
        compiler_params=_cparams("arbitrary"),
        name="repack_in",
    )(a)


def _pack_in_weights(w_in, b_in, layer):
    w_all, gates_w = _repack(w_in, layer, MXU_DTYPE)
    b_all, gates_b = _repack(b_in[:, None, :], layer, F32)
    scale = np.ones((1, W_ALL), np.float32)
    qs = HEAD_DIM ** -0.5
    scale[:, OFF_QK:OFF_QK + HW] = qs * LOG2E
    scale[:, OFF_SM + SM_W:OFF_SM + SM_W + IDX_HEADS] = IDX_HEADS ** -0.5 * IDX_DIM ** -0.5
    scale[:, OFF_B:OFF_B + HW] = qs
    scale[:, OFF_C:OFF_C + HW] = qs * LOG2E
    return w_all, b_all, jnp.asarray(scale), gates_w, gates_b


def _cumsum_kernel(sm_ref, tri_ref, col_ref, row_ref, carry_ref):
    @pl.when(pl.program_id(1) == 0)
    def _():
        carry_ref[...] = jnp.zeros_like(carry_ref)

    x = sm_ref[0]
    lf = jnp.minimum(x, 0.0) - jnp.log(1.0 + jnp.exp(-jnp.abs(x)))
    cs = jnp.dot(tri_ref[...], lf, precision=lax.Precision.HIGHEST,
                 preferred_element_type=F32) + carry_ref[0:1, :]
    col_ref[0] = cs
    row_ref[0, 0] = cs.T[SM_F:SM_F + SUBLANES, :]
    carry_ref[...] = jnp.broadcast_to(cs[-1:, :], carry_ref.shape)


def _forget_cumsum(sm, tq):
    b, s, _ = sm.shape
    tri = jnp.asarray(np.tril(np.ones((tq, tq), np.float32)))
    return pl.pallas_call(
        _cumsum_kernel,
        grid=(b, s // tq),
        in_specs=[pl.BlockSpec((1, tq, LANES), lambda bi, i: (bi, i, 0)),
                  pl.BlockSpec((tq, tq), lambda bi, i: (0, 0))],
        out_specs=[pl.BlockSpec((1, tq, LANES), lambda bi, i: (bi, i, 0)),
                   pl.BlockSpec((1, 1, SUBLANES, tq), lambda bi, i: (bi, i, 0, 0))],
        out_shape=[jax.ShapeDtypeStruct((b, s, LANES), F32),
                   jax.ShapeDtypeStruct((b, s // tq, SUBLANES, tq), F32)],
        scratch_shapes=[pltpu.VMEM((SUBLANES, LANES), F32)],
        compiler_params=_cparams("arbitrary", "arbitrary"),
        name="forget_cumsum",
    )(sm, tri)


def _store_masked_qt(qm_ref, qt_ref, n_heads):
    zero = jnp.zeros((HEAD_DIM, qm_ref.shape[2]), qm_ref.dtype)
    for h in range(n_heads):
        qh = qt_ref[h * HEAD_DIM:(h + 1) * HEAD_DIM, :]
        qm_ref[h] = jnp.concatenate([qh, zero] if h % 2 == 0 else [zero, qh], axis=0)


def _softmax_heads(n_heads, scores_fn, vt_fn, ms, acc_ref, s_ref):
    new_ms, alphas, pvs = [], [], []
    slots = s_ref.shape[0]
    for h in range(min(slots - 1, n_heads)):
        s_ref[h] = scores_fn(h)
    for h in range(n_heads):
        if h + slots - 1 < n_heads:
            s_ref[(h + slots - 1) % slots] = scores_fn(h + slots - 1)
        s = s_ref[h % slots]
        m_new = jnp.maximum(ms[h], jnp.max(s, axis=0, keepdims=True))
        alphas.append(jnp.exp2(ms[h] - m_new))
        pe = jnp.exp2(s - m_new)
        new_ms.append(m_new)
        vt = vt_fn(h)
        pvs.append(_dot(vt, pe.astype(vt.dtype)))
        if h > 0:
            acc_ref[h - 1] = acc_ref[h - 1] * alphas[h - 1] + pvs[h - 1]
    h = n_heads - 1
    acc_ref[h] = acc_ref[h] * alphas[h] + pvs[h]
    return tuple(new_ms)


def _finish_heads(o_ref, acc_ref, n_heads):
    outs = [acc_ref[h, :HEAD_DIM, :] / acc_ref[h, HEAD_DIM:HEAD_DIM + 1, :] for h in range(n_heads)]
    pad = HW // HEAD_DIM - n_heads
    if pad:
        outs.append(jnp.zeros((pad * HEAD_DIM, acc_ref.shape[2]), F32))
    o_ref[0] = jnp.concatenate(outs, axis=0).T.astype(o_ref.dtype)


def _dsa_kernel(qit_ref, ki_ref, wt_ref, qt_ref, k_ref, vt_ref, tri_ref, o_ref,
                keys_ref, top_ref, qm_ref, acc_ref, s_ref, *, tq, tk, topk):
    i = pl.program_id(1)
    nkb = (i * tq + tq - 1) // tk + 1
    q_pos = i * tq + lax.broadcasted_iota(jnp.int32, (1, tq), 1)
    q_chunk = q_pos // CHUNK
    _store_masked_qt(qm_ref, qt_ref, N_HEADS_A)

    def store_scores(kb, last):
        k_blk = ki_ref[0, pl.ds(pl.multiple_of(kb * tk, tk), tk), :]
        acc = jnp.zeros((tk, tq), F32)
        for h in range(IDX_HEADS):
            s = _dot(k_blk, qit_ref[h * IDX_DIM:(h + 1) * IDX_DIM, :])
            acc = acc + wt_ref[h:h + 1, :] * jnp.maximum(s, 0.0)
        if last:
            k_chunk = (kb * tk + lax.broadcasted_iota(jnp.int32, (tk, 1), 0)) // CHUNK
            acc = jnp.where(k_chunk <= q_chunk, acc, -jnp.inf)
        bits = lax.bitcast_convert_type(acc, jnp.int32)
        keys = bits ^ ((bits >> 31) & jnp.int32(0x7FFFFFFF))
        keys_ref[kb] = keys
        top_ref[kb] = (keys >> 16).astype(top_ref.dtype)

    def score_body(kb, carry):
        store_scores(kb, False)
        return carry

    lax.fori_loop(0, nkb - 1, score_body, 0)
    store_scores(nkb - 1, True)

    kcap = jnp.minimum(topk, (q_chunk + 1) * CHUNK)

    def count(pred, ref=keys_ref):
        dtype = ref.dtype
        one, zero = jnp.ones((), dtype), jnp.zeros((), dtype)

        def body(kb, acc):
            for c in range(tk // LANES):
                hit = jnp.where(pred(ref[kb, c * LANES:(c + 1) * LANES, :]), one, zero)
                parts = [hit[r:r + COUNT_ROWS] for r in range(0, LANES, COUNT_ROWS)]
                acc = acc + ((parts[0] + parts[1]) + (parts[2] + parts[3]))
            return acc

        acc = lax.fori_loop(0, nkb, body, jnp.zeros((COUNT_ROWS, tq), dtype))
        return jnp.sum(acc.astype(jnp.int32), axis=0, keepdims=True)

    def coarse_body(t, prefix):
        cand_u = prefix | (jnp.int32(1) << (15 - t))
        cand = (cand_u - jnp.int32(1 << 15)).astype(top_ref.dtype)
        return jnp.where(count(lambda blk: blk >= cand, top_ref) >= kcap, cand_u, prefix)

    top = lax.fori_loop(0, 16, coarse_body, jnp.zeros((1, tq), jnp.int32)) - jnp.int32(1 << 15)
    top16 = top.astype(top_ref.dtype)
    n_above = count(lambda blk: blk > top16, top_ref)

    lowest = jnp.full((), -(1 << 15), top_ref.dtype)

    def narrow_body(kb, carry):
        for c in range(tk // LANES):
            rows = slice(c * LANES, (c + 1) * LANES)
            low = (keys_ref[kb, rows, :] & jnp.int32(0xFFFF)) - jnp.int32(1 << 15)
            top_ref[kb, rows, :] = jnp.where(top_ref[kb, rows, :] == top16, low.astype(top_ref.dtype), lowest)
        return carry

    lax.fori_loop(0, nkb, narrow_body, 0)

    def fine_body(t, prefix):
        cand_u = prefix | (jnp.int32(1) << (15 - t))
        cand = (cand_u - jnp.int32(1 << 15)).astype(top_ref.dtype)
        n_ge = n_above + count(lambda blk: blk >= cand, top_ref)
        return jnp.where(n_ge >= kcap, cand_u, prefix)

    thr = (top << 16) | lax.fori_loop(0, 16, fine_body, jnp.zeros((1, tq), jnp.int32))
    n_gt = count(lambda blk: blk > thr)
    n_eq = count(lambda blk: blk >= thr) - n_gt
    no_cut = jnp.min(jnp.where(kcap - n_gt == n_eq, 1, 0)) == 1
    need = (kcap - n_gt).astype(F32)

    acc_ref[...] = jnp.zeros(acc_ref.shape, F32)
    row = lambda v: jnp.full((1, tq), v, F32)

    def attend(ranked):
        def body(kb, carry):
            tie_seen, ms = carry
            blk = keys_ref[kb]
            if ranked:
                eq = blk == thr
                rank = _dot(tri_ref[...], jnp.where(eq, 1.0, 0.0).astype(tri_ref.dtype))
                keep = (blk > thr) | (eq & (rank + tie_seen <= need))
                tie_seen = tie_seen + rank[tk - 1:tk, :]
            else:
                keep = blk >= thr
            bias = jnp.where(keep, 0.0, -jnp.inf)
            start = pl.multiple_of(kb * tk, tk)

            def scores(h):
                p = h // 2
                return _dot(k_ref[0, pl.ds(start, tk), p * LANES:(p + 1) * LANES], qm_ref[h]) + bias

            ms = _softmax_heads(N_HEADS_A, scores,
                                lambda h: vt_ref[0, kb, h * V_ROWS:(h + 1) * V_ROWS, :],
                                ms, acc_ref, s_ref)
            return tie_seen, ms

        return lax.fori_loop(0, nkb, body, (row(0.0), (row(NEG_BIG),) * N_HEADS_A))[0]

    lax.cond(no_cut, lambda: attend(False), lambda: attend(True))
    _finish_heads(o_ref, acc_ref, N_HEADS_A)


def _dsa_attention(qt, k, vt, qit, ki, wt, topk, tq, tk):
    b, s, _ = k.shape
    nq, nk = s // tq, s // tk
    nqi = IDX_HEADS * IDX_DIM
    tri = jnp.asarray(np.tril(np.ones((tk, tk), np.float32))).astype(MXU_DTYPE)
    feat = lambda n: pl.BlockSpec((n, tq), lambda bi, i: (0, bi * nq + i))
    full = lambda n: pl.BlockSpec((1, s, n), lambda bi, i: (bi, 0, 0), pipeline_mode=pl.Buffered(1))
    vrows = N_HEADS_A * V_ROWS
    return pl.pallas_call(
        functools.partial(_dsa_kernel, tq=tq, tk=tk, topk=topk),
        grid=(b, nq),
        in_specs=[feat(nqi), full(IDX_DIM), feat(IDX_HEADS), feat(HW), full(HW),
                  pl.BlockSpec((1, nk, vrows, tk), lambda bi, i: (bi, 0, 0, 0),
                               pipeline_mode=pl.Buffered(1)),
                  pl.BlockSpec((tk, tk), lambda bi, i: (0, 0))],
        out_specs=pl.BlockSpec((1, tq, HW), lambda bi, i: (bi, i, 0)),
        out_shape=jax.ShapeDtypeStruct((b, s, HW), MXU_DTYPE),
        scratch_shapes=[pltpu.VMEM((nk, tk, tq), jnp.int32),
                        pltpu.VMEM((nk, tk, tq), jnp.int16),
                        pltpu.VMEM((N_HEADS_A, LANES, tq), MXU_DTYPE),
                        pltpu.VMEM((N_HEADS_A, V_ROWS, tq), F32),
                        pltpu.VMEM((SCORE_SLOTS, tk, tq), F32)],
        compiler_params=_cparams("arbitrary", "arbitrary"),
        name="dsa_attention",
    )(qit, ki, wt, qt, k, vt.reshape(b, nk, vrows, tk), tri)


def _forget_kernel(qt_ref, k_ref, vt_ref, fcol_ref, frow_ref, o_ref,
                   qm_ref, acc_ref, s_ref, *, tq, tk):
    i = pl.program_id(1)
    nkb = (i * tq + tq - 1) // tk + 1
    q_pos = i * tq + lax.broadcasted_iota(jnp.int32, (1, tq), 1)
    _store_masked_qt(qm_ref, qt_ref, N_HEADS_C)
    f_q = frow_ref[0, 0] * LOG2E
    acc_ref[...] = jnp.zeros(acc_ref.shape, F32)

    def step(kb, ms, last):
        start = pl.multiple_of(kb * tk, tk)
        f_k = fcol_ref[0, pl.ds(start, tk), :] * LOG2E
        if last:
            causal = kb * tk + lax.broadcasted_iota(jnp.int32, (tk, 1), 0) <= q_pos

        def scores(h):
            p = h // 2
            s = _dot(k_ref[0, pl.ds(start, tk), p * LANES:(p + 1) * LANES], qm_ref[h])
            s = s + (f_q[h:h + 1, :] - f_k[:, SM_F + h:SM_F + h + 1])
            return jnp.where(causal, s, -jnp.inf) if last else s

        return _softmax_heads(N_HEADS_C, scores,
                              lambda h: vt_ref[0, kb, h * V_ROWS:(h + 1) * V_ROWS, :],
                              ms, acc_ref, s_ref)

    ms = lax.fori_loop(0, nkb - 1, lambda kb, ms: step(kb, ms, False),
                       (jnp.full((1, tq), NEG_BIG, F32),) * N_HEADS_C)
    step(nkb - 1, ms, True)
    _finish_heads(o_ref, acc_ref, N_HEADS_C)


def _forget_attention(qt, k, vt, fcol, frow, tq, tk):
    b, s, _ = k.shape
    nq, nk = s // tq, s // tk
    vrows = N_HEADS_C * V_ROWS
    return pl.pallas_call(
        functools.partial(_forget_kernel, tq=tq, tk=tk),
        grid=(b, nq),
        in_specs=[pl.BlockSpec((HW, tq), lambda bi, i: (0, bi * nq + i)),
                  pl.BlockSpec((1, s, HW), lambda bi, i: (bi, 0, 0), pipeline_mode=pl.Buffered(1)),
                  pl.BlockSpec((1, nk, vrows, tk), lambda bi, i: (bi, 0, 0, 0),
                               pipeline_mode=pl.Buffered(1)),
                  pl.BlockSpec((1, s, LANES), lambda bi, i: (bi, 0, 0), pipeline_mode=pl.Buffered(1)),
                  pl.BlockSpec((1, 1, SUBLANES, tq), lambda bi, i: (bi, i, 0, 0))],
        out_specs=pl.BlockSpec((1, tq, HW), lambda bi, i: (bi, i, 0)),
        out_shape=jax.ShapeDtypeStruct((b, s, HW), MXU_DTYPE),
        scratch_shapes=[pltpu.VMEM((N_HEADS_C, LANES, tq), MXU_DTYPE),
                        pltpu.VMEM((N_HEADS_C, V_ROWS, tq), F32),
                        pltpu.VMEM((SCORE_SLOTS, tk, tq), F32)],
        compiler_params=_cparams("arbitrary", "arbitrary"),
        name="forget_attention",
    )(qt, k, vt.reshape(b, nk, vrows, tk), fcol, frow)


def _band_kernel(qt_ref, k_ref, vt_ref, bt_ref, o_ref, qm_ref, s_ref, *, tq):
    i = pl.program_id(1)
    win = tq + BAND_PAD
    start = pl.multiple_of(i * tq, tq)
    _store_masked_qt(qm_ref, qt_ref, N_HEADS_B)
    key_pos = start - BAND_PAD + lax.broadcasted_iota(jnp.int32, (win, 1), 0)
    in_seq = key_pos >= 0
    for h in range(N_HEADS_B):
        p = h // 2
        s = _dot(k_ref[0, pl.ds(start, win), p * LANES:(p + 1) * LANES], qm_ref[h]) + bt_ref[h]
        s_ref[h] = jnp.where(in_seq, s, NEG_BIG)
    outs = []
    for h in range(N_HEADS_B):
        s = s_ref[h]
        pe = jnp.exp(s - jnp.max(s, axis=0, keepdims=True)).astype(vt_ref.dtype)
        pv = None
        for j in range(win // BAND_KEYS):
            term = _dot(vt_ref[0, i * (tq // BAND_KEYS) + j, h * V_ROWS:(h + 1) * V_ROWS, :],
                        pe[j * BAND_KEYS:(j + 1) * BAND_KEYS, :])
            pv = term if pv is None else pv + term
        outs.append(pv[:HEAD_DIM] / pv[HEAD_DIM:HEAD_DIM + 1])
    outs.append(jnp.zeros((HW - N_HEADS_B * HEAD_DIM, tq), F32))
    o_ref[0] = jnp.concatenate(outs, axis=0).T.astype(o_ref.dtype)


def _band_bias_kernel(rb_ref, o_ref, *, tq):
    h = pl.program_id(0)
    win = tq + BAND_PAD
    c = lax.broadcasted_iota(jnp.int32, (win, tq), 0)
    r = lax.broadcasted_iota(jnp.int32, (win, tq), 1)
    dist = jnp.clip(r + BAND_PAD - c, -REL_CLIP, REL_CLIP) + REL_CLIP
    back = r // CHUNK + PREV_CHUNKS - c // CHUNK
    table = lax.fori_loop(0, 2 * REL_CLIP + 1,
                          lambda j, acc: jnp.where(dist == j, rb_ref[h, j], acc),
                          jnp.zeros((win, tq), F32))
    o_ref[0] = jnp.where((back >= 0) & (back <= PREV_CHUNKS), table, NEG_BIG)


def _band_attention(qt, k, vt, rel_bias):
    b, s, _ = k.shape
    tq = min(128, s)
    win = tq + BAND_PAD
    nq = s // tq
    vrows = N_HEADS_B * V_ROWS
    pad_blocks = BAND_PAD // BAND_KEYS
    k = jnp.pad(k, ((0, 0), (BAND_PAD, 0), (0, 0)))
    vt = jnp.pad(vt.reshape(b, s // BAND_KEYS, vrows, BAND_KEYS), ((0, 0), (pad_blocks, 0), (0, 0), (0, 0)))
    bt = pl.pallas_call(
        functools.partial(_band_bias_kernel, tq=tq),
        grid=(N_HEADS_B,),
        in_specs=[pl.BlockSpec(memory_space=pltpu.SMEM)],
        out_specs=pl.BlockSpec((1, win, tq), lambda h: (h, 0, 0)),
        out_shape=jax.ShapeDtypeStruct((N_HEADS_B, win, tq), F32),
        compiler_params=_cparams("arbitrary"),
        name="band_bias",
    )(rel_bias)
    return pl.pallas_call(
        functools.partial(_band_kernel, tq=tq),
        grid=(b, nq),
        in_specs=[pl.BlockSpec((HW, tq), lambda bi, i: (0, bi * nq + i)),
                  pl.BlockSpec((1, s + BAND_PAD, HW), lambda bi, i: (bi, 0, 0)),
                  pl.BlockSpec((1, s // BAND_KEYS + pad_blocks, vrows, BAND_KEYS), lambda bi, i: (bi, 0, 0, 0)),
                  pl.BlockSpec((N_HEADS_B, win, tq), lambda bi, i: (0, 0, 0))],
        out_specs=pl.BlockSpec((1, tq, HW), lambda bi, i: (bi, i, 0)),
        out_shape=jax.ShapeDtypeStruct((b, s, HW), MXU_DTYPE),
        scratch_shapes=[pltpu.VMEM((N_HEADS_B, LANES, tq), MXU_DTYPE),
                        pltpu.VMEM((N_HEADS_B, win, tq), F32)],
        compiler_params=_cparams("arbitrary", "arbitrary"),
        name="band_attention",
    )(qt, k, vt, bt)


def _rms_mod(y, ng_ref, sc_ref, sh_ref):
    n = y * lax.rsqrt(jnp.mean(y * y, axis=-1, keepdims=True) + RMS_EPS) * ng_ref[...]
    return n if sc_ref is None else n * (1.0 + sc_ref[0]) + sh_ref[0]


def _merge_kernel(x_ref, h_ref, oa_ref, ob_ref, oc_ref, wg_ref, bg_ref, wb_ref, wo_ref, g_ref,
                  ng_ref, sc_ref, sh_ref, o_ref, hn_ref):
    h = h_ref[0]
    d = h.shape[1]
    y = None
    for br, o_br in enumerate((oa_ref, ob_ref, oc_ref)):
        gate = _sigmoid(_dot(h, wg_ref[:, br * d:(br + 1) * d]) + bg_ref[:, br * d:(br + 1) * d])
        term = gate * _dot(o_br[0], wb_ref[br])
        y = term if y is None else y + term
    x_new = x_ref[0] + g_ref[0] * _dot(y.astype(wo_ref.dtype), wo_ref[...])
    o_ref[0] = x_new
    hn_ref[0] = _rms_mod(x_new, ng_ref, sc_ref, sh_ref).astype(hn_ref.dtype)


def _merge(x, h, o_a, o_b, o_c, gates_w, gates_b, wb, w_out, mod, norm_g):
    b, s, d = x.shape
    tm = min(s, 512)
    tok = lambda n: pl.BlockSpec((1, tm, n), lambda bi, i: (bi, i, 0))
    const2 = lambda shape: pl.BlockSpec(shape, lambda bi, i: (0, 0))
    modblk = lambda j: pl.BlockSpec((1, 1, d), lambda bi, i: (bi, 0, j))
    return pl.pallas_call(
        _merge_kernel,
        grid=(b, s // tm),
        in_specs=[tok(d), tok(d), tok(HW), tok(HW), tok(HW),
                  const2((d, N_BRANCH * d)), const2((1, N_BRANCH * d)),
                  pl.BlockSpec((N_BRANCH, HW, d), lambda bi, i: (0, 0, 0)),
                  const2((d, d)), modblk(2), const2((1, d)), modblk(4), modblk(3)],
        out_specs=[tok(d), tok(d)],
        out_shape=[jax.ShapeDtypeStruct((b, s, d), F32), jax.ShapeDtypeStruct((b, s, d), MXU_DTYPE)],
        compiler_params=_cparams("arbitrary", "arbitrary"),
        name="gated_merge",
    )(x, h, o_a, o_b, o_c, gates_w, gates_b, wb, w_out, mod, norm_g.reshape(1, d), mod, mod)


def _ffn_kernel(x_ref, h_ref, wi_ref, wo_ref, g_ref, ng_ref, *rest, last):
    h = h_ref[0]
    g = _dot(h, wi_ref[:, :FFN_HIDDEN])
    u = _dot(h, wi_ref[:, FFN_HIDDEN:])
    act = (g * _sigmoid(g)) * u
    x_new = x_ref[0] + g_ref[0] * _dot(act.astype(wo_ref.dtype), wo_ref[...])
    if last:
        (o_ref,) = rest
        o_ref[0] = _rms_mod(x_new, ng_ref, None, None).astype(o_ref.dtype)
    else:
        sc_ref, sh_ref, o_ref, hn_ref = rest
        o_ref[0] = x_new
        hn_ref[0] = _rms_mod(x_new, ng_ref, sc_ref, sh_ref).astype(hn_ref.dtype)


def _ffn(x, h, w_in, w_out, mod, norm_g, next_mod):
    b, s, d = x.shape
    tm = min(s, 512)
    last = next_mod is None
    tok = pl.BlockSpec((1, tm, d), lambda bi, i: (bi, i, 0))
    resident = lambda shape: pl.BlockSpec(shape, lambda bi, i: (0, 0), pipeline_mode=pl.Buffered(1))
    modblk = lambda j: pl.BlockSpec((1, 1, d), lambda bi, i: (bi, 0, j))
    in_specs = [tok, tok, resident(w_in.shape), resident(w_out.shape), modblk(5),
                pl.BlockSpec((1, d), lambda bi, i: (0, 0))]
    args = [x, h, w_in, w_out, mod, norm_g.reshape(1, d)]
    if last:
        out_specs, out_shape = tok, jax.ShapeDtypeStruct((b, s, d), F32)
    else:
        in_specs += [modblk(1), modblk(0)]
        args += [next_mod, next_mod]
        out_specs = [tok, tok]
        out_shape = [jax.ShapeDtypeStruct((b, s, d), F32), jax.ShapeDtypeStruct((b, s, d), MXU_DTYPE)]
    return pl.pallas_call(
        functools.partial(_ffn_kernel, last=last),
        grid=(b, s // tm),
        in_specs=in_specs,
        out_specs=out_specs,
        out_shape=out_shape,
        compiler_params=_cparams("arbitrary", "arbitrary"),
        name="swiglu_ffn",
    )(*args)


def _pack_branch_weights(w_branch):
    pad = lambda a: jnp.pad(a, ((0, HW - a.shape[0]), (0, 0)))
    return jnp.stack([pad(w_branch[:WIDTH_A]),
                      pad(w_branch[WIDTH_A:WIDTH_A + WIDTH_B]),
                      pad(w_branch[WIDTH_A + WIDTH_B:])]).astype(MXU_DTYPE)


def kernel(x, c, positions, ada_w, ada_b, norm1_g, w_in, b_in, rel_bias, w_branch, w_out,
           norm2_g, w_ffn_in, w_ffn_out, final_g):
    b, s, d = x.shape
    depth = ada_w.shape[0]
    topk = min(TOPK_MAX, s // 4)
    tq, tk = min(Q_BLOCK, s), min(K_BLOCK, s)
    cos, sa, sb = _rope_tables(positions)
    mod_all = _ada_mod(c, ada_w, ada_b)
    mods = [mod_all[l].reshape(b, 1, 6 * d) for l in range(depth)]
    h = _norm(x, norm1_g[0], mods[0], 1, 0, MXU_DTYPE)
    for l in range(depth):
        mod = mods[l]
        w_all, b_all, s_all, gates_w, gates_b = _pack_in_weights(w_in, b_in, l)
        (qt_a, k_a, vt_a, qit, ki, wt, sm, qt_b, k_b, vt_b, qt_c, k_c, vt_c) = _proj(
            h.reshape(b * s, d), w_all, b_all, s_all, cos, sa, sb, tk)
        k_a, ki, sm, k_b, k_c = (a.reshape(b, s, a.shape[-1]) for a in (k_a, ki, sm, k_b, k_c))
        o_a = _dsa_attention(qt_a, k_a, vt_a, qit, ki, wt, topk, tq, tk)
        o_b = _band_attention(qt_b, k_b, vt_b, rel_bias[l])
        fcol, frow = _forget_cumsum(sm, tq)
        o_c = _forget_attention(qt_c, k_c, vt_c, fcol, frow, tq, tk)
        x, h = _merge(x, h, o_a, o_b, o_c, gates_w, gates_b, _pack_branch_weights(w_branch[l]),
                      w_out[l].astype(MXU_DTYPE), mod, norm2_g[l])
        ffn_w = (w_ffn_in[l].astype(MXU_DTYPE), w_ffn_out[l].astype(MXU_DTYPE))
        if l + 1 < depth:
            x, h = _ffn(x, h, *ffn_w, mod, norm1_g[l + 1], mods[l + 1])
        else:
            x = _ffn(x, h, *ffn_w, mod, final_g, None)
    return x
```

```python
import functools
import math

import numpy as np
import jax
import jax.numpy as jnp
from jax import lax
from jax.experimental import pallas as pl
from jax.experimental.pallas import tpu as pltpu

D_MODEL = 1024
CHUNK = 64
HEAD_DIM = 64
ROT_DIM = HEAD_DIM // 4
ROPE_THETA = 500000.0
N_HEADS_A = 6
N_HEADS_B = 5
N_HEADS_C = 5
WIDTH_A = N_HEADS_A * HEAD_DIM
WIDTH_B = N_HEADS_B * HEAD_DIM
WIDTH_C = N_HEADS_C * HEAD_DIM
IDX_HEADS = 8
IDX_DIM = 64
TOPK_MAX = 256
PREV_CHUNKS = 8
REL_CLIP = 128
N_BRANCH = 3
FFN_HIDDEN = -(-8 * D_MODEL // (3 * 256)) * 256
RMS_EPS = 1e-6
SPLIT_SIZES = (WIDTH_A, WIDTH_A, WIDTH_A, IDX_HEADS * IDX_DIM, IDX_DIM, IDX_HEADS,
               3 * WIDTH_B, 3 * WIDTH_C, N_HEADS_C, N_BRANCH * D_MODEL)

LANES = 128
SUBLANES = 8
PACKED_ROWS = 16
HW = 384
BAND_PAD = PREV_CHUNKS * CHUNK
VMEM_LIMIT = 56 * 1024 * 1024

MXU_DTYPE = jnp.bfloat16
F32 = jnp.float32
NEG_BIG = -1e30
INT_MIN = -2 ** 31
LOG2E = math.log2(math.e)

OFF_QK = 0
OFF_QI = 2 * HW
OFF_KI = OFF_QI + IDX_HEADS * IDX_DIM
OFF_SM = OFF_KI + LANES
OFF_VA = OFF_SM + LANES
OFF_B = OFF_VA + HW
OFF_C = OFF_B + 3 * HW
W_ALL = OFF_C + 3 * HW
SM_W = 0
SM_F = 8
Q_BLOCK = 512
K_BLOCK = 512
V_ROWS = HEAD_DIM + PACKED_ROWS
COUNT_ROWS = 4 * SUBLANES
BAND_KEYS = 128
SCORE_SLOTS = 6


def _cparams(*sem):
    return pltpu.CompilerParams(dimension_semantics=sem, vmem_limit_bytes=VMEM_LIMIT)


def _dot(a, b):
    return jnp.dot(a, b, preferred_element_type=F32)


def _dot_nt(a, b):
    return lax.dot_general(a, b, (((1,), (1,)), ((), ())), preferred_element_type=F32)


def _sigmoid(x):
    return 1.0 / (1.0 + jnp.exp(-x))


def _rope_kernel(pos_ref, invf_ref, ma_ref, mb_ref, cos_ref, sa_ref, sb_ref):
    ang = pos_ref[...].astype(F32) * invf_ref[...]
    s = jnp.sin(ang)
    cos_ref[...] = jnp.cos(ang)
    sa_ref[...] = s * ma_ref[...]
    sb_ref[...] = s * mb_ref[...]


def _rope_tables(positions):
    t = positions.size
    tm = min(t, 1024)
    half = ROT_DIM // 2
    d = np.arange(LANES) % HEAD_DIM
    inv = (ROPE_THETA ** (-(2.0 * (d % half)) / ROT_DIM)).astype(np.float32)
    invf = np.where(d < ROT_DIM, inv, 0.0).astype(np.float32)[None, :]
    ma = np.where(d < half, -1.0, 0.0).astype(np.float32)[None, :]
    mb = np.where((d >= half) & (d < ROT_DIM), 1.0, 0.0).astype(np.float32)[None, :]
    row = pl.BlockSpec((1, LANES), lambda i: (0, 0))
    tab = pl.BlockSpec((tm, LANES), lambda i: (i, 0))
    return pl.pallas_call(
        _rope_kernel,
        grid=(t // tm,),
        in_specs=[pl.BlockSpec((tm, 1), lambda i: (i, 0)), row, row, row],
        out_specs=[tab, tab, tab],
        out_shape=[jax.ShapeDtypeStruct((t, LANES), F32)] * 3,
        compiler_params=_cparams("arbitrary"),
        name="rope_tables",
    )(positions.reshape(t, 1), jnp.asarray(invf), jnp.asarray(ma), jnp.asarray(mb))


def _ada_kernel(c_ref, w_ref, b_ref, o_ref):
    c = c_ref[...]
    cond = c * _sigmoid(c)
    o_ref[0] = jnp.dot(cond, w_ref[0], precision=lax.Precision.HIGHEST,
                       preferred_element_type=F32) + b_ref[0]


def _ada_mod(c, ada_w, ada_b):
    depth, d, n = ada_w.shape
    b = c.shape[0]
    rows = -(-b // SUBLANES) * SUBLANES
    cp = jnp.pad(c, ((0, rows - b), (0, 0)))
    tn = 1024
    out = pl.pallas_call(
        _ada_kernel,
        grid=(depth, n // tn),
        in_specs=[pl.BlockSpec((rows, d), lambda l, j: (0, 0)),
                  pl.BlockSpec((1, d, tn), lambda l, j: (l, 0, j)),
                  pl.BlockSpec((1, 1, tn), lambda l, j: (l, 0, j))],
        out_specs=pl.BlockSpec((1, rows, tn), lambda l, j: (l, 0, j)),
        out_shape=jax.ShapeDtypeStruct((depth, rows, n), F32),
        compiler_params=_cparams("arbitrary", "arbitrary"),
        name="ada_mod",
    )(cp, ada_w, ada_b.reshape(depth, 1, n))
    return out[:, :b]


def _norm_kernel(x_ref, g_ref, *rest, modulate):
    x = x_ref[0]
    y = x * lax.rsqrt(jnp.mean(x * x, axis=-1, keepdims=True) + RMS_EPS) * g_ref[...]
    if modulate:
        sc_ref, sh_ref, o_ref = rest
        y = y * (1.0 + sc_ref[0]) + sh_ref[0]
    else:
        (o_ref,) = rest
    o_ref[0] = y.astype(o_ref.dtype)


def _norm(x, g, mod, sc_blk, sh_blk, out_dtype):
    b, s, d = x.shape
    tm = min(s, 512)
    xspec = pl.BlockSpec((1, tm, d), lambda bi, i: (bi, i, 0))
    in_specs = [xspec, pl.BlockSpec((1, d), lambda bi, i: (0, 0))]
    args = [x, g.reshape(1, d)]
    if mod is not None:
        in_specs += [pl.BlockSpec((1, 1, d), lambda bi, i: (bi, 0, sc_blk)),
                     pl.BlockSpec((1, 1, d), lambda bi, i: (bi, 0, sh_blk))]
        args += [mod, mod]
    return pl.pallas_call(
        functools.partial(_norm_kernel, modulate=mod is not None),
        grid=(b, s // tm),
        in_specs=in_specs,
        out_specs=xspec,
        out_shape=jax.ShapeDtypeStruct((b, s, d), out_dtype),
        compiler_params=_cparams("arbitrary", "arbitrary"),
        name="norm_mod" if mod is not None else "norm_final",
    )(*args)


def _proj_kernel(h_ref, w_ref, b_ref, s_ref, cos_ref, sa_ref, sb_ref,
                 qta_ref, ka_ref, vta_ref, qit_ref, ki_ref, wt_ref, sm_ref,
                 qtb_ref, kb_ref, vtb_ref, qtc_ref, kc_ref, vtc_ref, *, tk):
    h = h_ref[...]
    tm = h.shape[0]

    def seg(a, n):
        return (_dot(h, w_ref[:, a:a + n]) + b_ref[:, a:a + n]) * s_ref[:, a:a + n]

    def rope(a, n):
        r = seg(a, n)
        cos, sa, sb = cos_ref[...], sa_ref[...], sb_ref[...]
        out = []
        for c in range(n // LANES):
            x = r[:, c * LANES:(c + 1) * LANES]
            out.append(x * cos + pltpu.roll(x, LANES - ROT_DIM // 2, 1) * sa
                       + pltpu.roll(x, ROT_DIM // 2, 1) * sb)
        return jnp.concatenate(out, axis=1)

    def store_vt(vt_ref, v, n_heads):
        blk = vt_ref.shape[2]
        ones = jnp.ones((V_ROWS - HEAD_DIM, blk), F32)
        for j in range(tm // blk):
            vt = v[j * blk:(j + 1) * blk, :].T
            rows = []
            for hd in range(n_heads):
                rows += [vt[hd * HEAD_DIM:(hd + 1) * HEAD_DIM, :], ones]
            vt_ref[j] = jnp.concatenate(rows, axis=0).astype(vt_ref.dtype)

    qta_ref[...] = rope(OFF_QK, HW).T.astype(qta_ref.dtype)
    ka_ref[...] = rope(OFF_QK + HW, HW).astype(ka_ref.dtype)
    qit_ref[...] = rope(OFF_QI, IDX_HEADS * IDX_DIM).T.astype(qit_ref.dtype)
    ki_ref[...] = rope(OFF_KI, LANES)[:, :IDX_DIM].astype(ki_ref.dtype)
    sm = seg(OFF_SM, LANES)
    sm_ref[...] = sm
    wt_ref[...] = sm.T[SM_W:SM_W + IDX_HEADS, :]
    store_vt(vta_ref, seg(OFF_VA, HW), N_HEADS_A)
    qtb_ref[...] = seg(OFF_B, HW).T.astype(qtb_ref.dtype)
    kb_ref[...] = seg(OFF_B + HW, HW).astype(kb_ref.dtype)
    store_vt(vtb_ref, seg(OFF_B + 2 * HW, HW), N_HEADS_B)
    qtc_ref[...] = seg(OFF_C, HW).T.astype(qtc_ref.dtype)
    kc_ref[...] = seg(OFF_C + HW, HW).astype(kc_ref.dtype)
    store_vt(vtc_ref, seg(OFF_C + 2 * HW, HW), N_HEADS_C)


def _proj(h2d, w_all, b_all, s_all, cos, sa, sb, tk):
    t, d = h2d.shape
    tm = max(min(t, 512), tk)
    nqi = IDX_HEADS * IDX_DIM
    const = lambda shape: pl.BlockSpec(shape, lambda i: (0, 0))
    tok = lambda n: pl.BlockSpec((tm, n), lambda i: (i, 0))
    feat = lambda n: pl.BlockSpec((n, tm), lambda i: (0, i))
    vts = lambda nh, blk: pl.BlockSpec((tm // blk, nh * V_ROWS, blk), lambda i: (i, 0, 0))
    act = lambda shape: jax.ShapeDtypeStruct(shape, MXU_DTYPE)
    vshape = lambda nh, blk: act((t // blk, nh * V_ROWS, blk))
    bk = min(BAND_KEYS, tm)
    return pl.pallas_call(
        functools.partial(_proj_kernel, tk=tk),
        grid=(t // tm,),
        in_specs=[tok(d), const((d, W_ALL)), const((1, W_ALL)), const((1, W_ALL)),
                  tok(LANES), tok(LANES), tok(LANES)],
        out_specs=[feat(HW), tok(HW), vts(N_HEADS_A, tk), feat(nqi), tok(IDX_DIM), feat(IDX_HEADS),
                   tok(LANES), feat(HW), tok(HW), vts(N_HEADS_B, bk),
                   feat(HW), tok(HW), vts(N_HEADS_C, tk)],
        out_shape=[act((HW, t)), act((t, HW)), vshape(N_HEADS_A, tk),
                   act((nqi, t)), act((t, IDX_DIM)),
                   jax.ShapeDtypeStruct((IDX_HEADS, t), F32),
                   jax.ShapeDtypeStruct((t, LANES), F32),
                   act((HW, t)), act((t, HW)), vshape(N_HEADS_B, bk),
                   act((HW, t)), act((t, HW)), vshape(N_HEADS_C, tk)],
        compiler_params=_cparams("arbitrary"),
        name="in_proj",
    )(h2d, w_all, b_all, s_all, cos, sa, sb)


_SRC = np.cumsum((0,) + SPLIT_SIZES)
_IN_SEGMENTS = (
    (_SRC[0], WIDTH_A, OFF_QK), (_SRC[1], WIDTH_A, OFF_QK + HW), (_SRC[2], WIDTH_A, OFF_VA),
    (_SRC[3], IDX_HEADS * IDX_DIM, OFF_QI), (_SRC[4], IDX_DIM, OFF_KI),
    (_SRC[5], IDX_HEADS, OFF_SM + SM_W), (_SRC[8], N_HEADS_C, OFF_SM + SM_F),
) + tuple((_SRC[6] + j * WIDTH_B, WIDTH_B, OFF_B + j * HW) for j in range(3)) \
  + tuple((_SRC[7] + j * WIDTH_C, WIDTH_C, OFF_C + j * HW) for j in range(3))


def _repack_kernel(w_ref, all_ref, gates_ref):
    all_ref[...] = jnp.zeros(all_ref.shape, all_ref.dtype)
    for src, width, dst in _IN_SEGMENTS:
        all_ref[:, dst:dst + width] = w_ref[0, :, int(src):int(src) + width].astype(all_ref.dtype)
    g0 = int(_SRC[9])
    gates_ref[...] = w_ref[0, :, g0:g0 + gates_ref.shape[1]].astype(gates_ref.dtype)


def _repack(a, layer, dtype):
    _, rows, n = a.shape
    tr = min(rows, LANES)
    ng = N_BRANCH * D_MODEL
    return pl.pallas_call(
        _repack_kernel,
        grid=(rows // tr,),
        in_specs=[pl.BlockSpec((1, tr, n), lambda i: (layer, i, 0))],
        out_specs=[pl.BlockSpec((tr, W_ALL), lambda i: (i, 0)), pl.BlockSpec((tr, ng), lambda i: (i, 0))],
        out_shape=[jax.ShapeDtypeStruct((rows, W_ALL), dtype), jax.ShapeDtypeStruct((rows, ng), dtype)],
        compiler_params=_cparams("arbitrary"),
        name="repack_in",
    )(a)


def _pack_in_weights(w_in, b_in, layer):
    w_all, gates_w = _repack(w_in, layer, MXU_DTYPE)
    b_all, gates_b = _repack(b_in[:, None, :], layer, F32)
    scale = np.ones((1, W_ALL), np.float32)
    qs = HEAD_DIM ** -0.5
    scale[:, OFF_QK:OFF_QK + HW] = qs * LOG2E
    scale[:, OFF_SM + SM_W:OFF_SM + SM_W + IDX_HEADS] = IDX_HEADS ** -0.5 * IDX_DIM ** -0.5
    scale[:, OFF_B:OFF_B + HW] = qs
    scale[:, OFF_C:OFF_C + HW] = qs * LOG2E
    return w_all, b_all, jnp.asarray(scale), gates_w, gates_b


def _cumsum_kernel(sm_ref, tri_ref, col_ref, row_ref, carry_ref):
    @pl.when(pl.program_id(1) == 0)
    def _():
        carry_ref[...] = jnp.zeros_like(carry_ref)

    x = sm_ref[0]
    lf = jnp.minimum(x, 0.0) - jnp.log(1.0 + jnp.exp(-jnp.abs(x)))
    cs = jnp.dot(tri_ref[...], lf, precision=lax.Precision.HIGHEST,
                 preferred_element_type=F32) + carry_ref[0:1, :]
    col_ref[0] = cs
    row_ref[0, 0] = cs.T[SM_F:SM_F + SUBLANES, :]
    carry_ref[...] = jnp.broadcast_to(cs[-1:, :], carry_ref.shape)


def _forget_cumsum(sm, tq):
    b, s, _ = sm.shape
    tri = jnp.asarray(np.tril(np.ones((tq, tq), np.float32)))
    return pl.pallas_call(
        _cumsum_kernel,
        grid=(b, s // tq),
        in_specs=[pl.BlockSpec((1, tq, LANES), lambda bi, i: (bi, i, 0)),
                  pl.BlockSpec((tq, tq), lambda bi, i: (0, 0))],
        out_specs=[pl.BlockSpec((1, tq, LANES), lambda bi, i: (bi, i, 0)),
                   pl.BlockSpec((1, 1, SUBLANES, tq), lambda bi, i: (bi, i, 0, 0))],
        out_shape=[jax.ShapeDtypeStruct((b, s, LANES), F32),
                   jax.ShapeDtypeStruct((b, s // tq, SUBLANES, tq), F32)],
        scratch_shapes=[pltpu.VMEM((SUBLANES, LANES), F32)],
        compiler_params=_cparams("arbitrary", "arbitrary"),
        name="forget_cumsum",
    )(sm, tri)


def _store_masked_qt(qm_ref, qt_ref, n_heads):
    zero = jnp.zeros((HEAD_DIM, qm_ref.shape[2]), qm_ref.dtype)
    for h in range(n_heads):
        qh = qt_ref[h * HEAD_DIM:(h + 1) * HEAD_DIM, :]
        qm_ref[h] = jnp.concatenate([qh, zero] if h % 2 == 0 else [zero, qh], axis=0)


def _softmax_heads(n_heads, scores_fn, vt_fn, ms, acc_ref, s_ref):
    new_ms, alphas, pvs = [], [], []
    slots = s_ref.shape[0]
    for h in range(min(slots - 1, n_heads)):
        s_ref[h] = scores_fn(h)
    for h in range(n_heads):
        if h + slots - 1 < n_heads:
            s_ref[(h + slots - 1) % slots] = scores_fn(h + slots - 1)
        s = s_ref[h % slots]
        m_new = jnp.maximum(ms[h], jnp.max(s, axis=0, keepdims=True))
        alphas.append(jnp.exp2(ms[h] - m_new))
        pe = jnp.exp2(s - m_new)
        new_ms.append(m_new)
        vt = vt_fn(h)
        pvs.append(_dot(vt, pe.astype(vt.dtype)))
        if h > 0:
            acc_ref[h - 1] = acc_ref[h - 1] * alphas[h - 1] + pvs[h - 1]
    h = n_heads - 1
    acc_ref[h] = acc_ref[h] * alphas[h] + pvs[h]
    return tuple(new_ms)


def _finish_heads(o_ref, acc_ref, n_heads):
    outs = [acc_ref[h, :HEAD_DIM, :] / acc_ref[h, HEAD_DIM:HEAD_DIM + 1, :] for h in range(n_heads)]
    pad = HW // HEAD_DIM - n_heads
    if pad:
        outs.append(jnp.zeros((pad * HEAD_DIM, acc_ref.shape[2]), F32))
    o_ref[0] = jnp.concatenate(outs, axis=0).T.astype(o_ref.dtype)


def _dsa_kernel(qit_ref, ki_ref, wt_ref, qt_ref, k_ref, vt_ref, tri_ref, o_ref,
                keys_ref, top_ref, qm_ref, acc_ref, s_ref, *, tq, tk, topk):
    i = pl.program_id(1)
    nkb = (i * tq + tq - 1) // tk + 1
    q_pos = i * tq + lax.broadcasted_iota(jnp.int32, (1, tq), 1)
    q_chunk = q_pos // CHUNK
    _store_masked_qt(qm_ref, qt_ref, N_HEADS_A)

    def store_scores(kb, last):
        k_blk = ki_ref[0, pl.ds(pl.multiple_of(kb * tk, tk), tk), :]
        acc = jnp.zeros((tk, tq), F32)
        for h in range(IDX_HEADS):
            s = _dot(k_blk, qit_ref[h * IDX_DIM:(h + 1) * IDX_DIM, :])
            acc = acc + wt_ref[h:h + 1, :] * jnp.maximum(s, 0.0)
        if last:
            k_chunk = (kb * tk + lax.broadcasted_iota(jnp.int32, (tk, 1), 0)) // CHUNK
            acc = jnp.where(k_chunk <= q_chunk, acc, -jnp.inf)
        bits = lax.bitcast_convert_type(acc, jnp.int32)
        keys = bits ^ ((bits >> 31) & jnp.int32(0x7FFFFFFF))
        keys_ref[kb] = keys
        top_ref[kb] = (keys >> 16).astype(top_ref.dtype)

    def score_body(kb, carry):
        store_scores(kb, False)
        return carry

    lax.fori_loop(0, nkb - 1, score_body, 0)
    store_scores(nkb - 1, True)

    kcap = jnp.minimum(topk, (q_chunk + 1) * CHUNK)

    def count(pred, ref=keys_ref):
        dtype = ref.dtype
        one, zero = jnp.ones((), dtype), jnp.zeros((), dtype)

        def body(kb, acc):
            for c in range(tk // LANES):
                hit = jnp.where(pred(ref[kb, c * LANES:(c + 1) * LANES, :]), one, zero)
                parts = [hit[r:r + COUNT_ROWS] for r in range(0, LANES, COUNT_ROWS)]
                acc = acc + ((parts[0] + parts[1]) + (parts[2] + parts[3]))
            return acc

        acc = lax.fori_loop(0, nkb, body, jnp.zeros((COUNT_ROWS, tq), dtype))
        return jnp.sum(acc.astype(jnp.int32), axis=0, keepdims=True)

    def coarse_body(t, prefix):
        cand_u = prefix | (jnp.int32(1) << (15 - t))
        cand = (cand_u - jnp.int32(1 << 15)).astype(top_ref.dtype)
        return jnp.where(count(lambda blk: blk >= cand, top_ref) >= kcap, cand_u, prefix)

    top = lax.fori_loop(0, 16, coarse_body, jnp.zeros((1, tq), jnp.int32)) - jnp.int32(1 << 15)
    top16 = top.astype(top_ref.dtype)
    n_above = count(lambda blk: blk > top16, top_ref)

    lowest = jnp.full((), -(1 << 15), top_ref.dtype)

    def narrow_body(kb, carry):
        for c in range(tk // LANES):
            rows = slice(c * LANES, (c + 1) * LANES)
            low = (keys_ref[kb, rows, :] & jnp.int32(0xFFFF)) - jnp.int32(1 << 15)
            top_ref[kb, rows, :] = jnp.where(top_ref[kb, rows, :] == top16, low.astype(top_ref.dtype), lowest)
        return carry

    lax.fori_loop(0, nkb, narrow_body, 0)

    def fine_body(t, prefix):
        cand_u = prefix | (jnp.int32(1) << (15 - t))
        cand = (cand_u - jnp.int32(1 << 15)).astype(top_ref.dtype)
        n_ge = n_above + count(lambda blk: blk >= cand, top_ref)
        return jnp.where(n_ge >= kcap, cand_u, prefix)

    thr = (top << 16) | lax.fori_loop(0, 16, fine_body, jnp.zeros((1, tq), jnp.int32))
    low16 = ((thr & jnp.int32(0xFFFF)) - jnp.int32(1 << 15)).astype(top_ref.dtype)
    n_gt = n_above + count(lambda blk: blk > low16, top_ref)
    n_eq = count(lambda blk: blk == low16, top_ref)
    no_cut = jnp.min(jnp.where(kcap - n_gt == n_eq, 1, 0)) == 1
    need = (kcap - n_gt).astype(F32)

    acc_ref[...] = jnp.zeros(acc_ref.shape, F32)
    row = lambda v: jnp.full((1, tq), v, F32)

    def attend(ranked):
        def body(kb, carry):
            tie_seen, ms = carry
            blk = keys_ref[kb]
            if ranked:
                eq = blk == thr
                rank = _dot(tri_ref[...], jnp.where(eq, 1.0, 0.0).astype(tri_ref.dtype))
                keep = (blk > thr) | (eq & (rank + tie_seen <= need))
                tie_seen = tie_seen + rank[tk - 1:tk, :]
            else:
                keep = blk >= thr
            bias = jnp.where(keep, 0.0, -jnp.inf)
            start = pl.multiple_of(kb * tk, tk)

            def scores(h):
                p = h // 2
                return _dot(k_ref[0, pl.ds(start, tk), p * LANES:(p + 1) * LANES], qm_ref[h]) + bias

            ms = _softmax_heads(N_HEADS_A, scores,
                                lambda h: vt_ref[0, kb, h * V_ROWS:(h + 1) * V_ROWS, :],
                                ms, acc_ref, s_ref)
            return tie_seen, ms

        return lax.fori_loop(0, nkb, body, (row(0.0), (row(NEG_BIG),) * N_HEADS_A))[0]

    lax.cond(no_cut, lambda: attend(False), lambda: attend(True))
    _finish_heads(o_ref, acc_ref, N_HEADS_A)


def _dsa_attention(qt, k, vt, qit, ki, wt, topk, tq, tk):
    b, s, _ = k.shape
    nq, nk = s // tq, s // tk
    nqi = IDX_HEADS * IDX_DIM
    tri = jnp.asarray(np.tril(np.ones((tk, tk), np.float32))).astype(MXU_DTYPE)
    feat = lambda n: pl.BlockSpec((n, tq), lambda bi, i: (0, bi * nq + i))
    full = lambda n: pl.BlockSpec((1, s, n), lambda bi, i: (bi, 0, 0), pipeline_mode=pl.Buffered(1))
    vrows = N_HEADS_A * V_ROWS
    return pl.pallas_call(
        functools.partial(_dsa_kernel, tq=tq, tk=tk, topk=topk),
        grid=(b, nq),
        in_specs=[feat(nqi), full(IDX_DIM), feat(IDX_HEADS), feat(HW), full(HW),
                  pl.BlockSpec((1, nk, vrows, tk), lambda bi, i: (bi, 0, 0, 0),
                               pipeline_mode=pl.Buffered(1)),
                  pl.BlockSpec((tk, tk), lambda bi, i: (0, 0))],
        out_specs=pl.BlockSpec((1, tq, HW), lambda bi, i: (bi, i, 0)),
        out_shape=jax.ShapeDtypeStruct((b, s, HW), MXU_DTYPE),
        scratch_shapes=[pltpu.VMEM((nk, tk, tq), jnp.int32),
                        pltpu.VMEM((nk, tk, tq), jnp.int16),
                        pltpu.VMEM((N_HEADS_A, LANES, tq), MXU_DTYPE),
                        pltpu.VMEM((N_HEADS_A, V_ROWS, tq), F32),
                        pltpu.VMEM((SCORE_SLOTS, tk, tq), F32)],
        compiler_params=_cparams("arbitrary", "arbitrary"),
        name="dsa_attention",
    )(qit, ki, wt, qt, k, vt.reshape(b, nk, vrows, tk), tri)


def _forget_kernel(qt_ref, k_ref, vt_ref, fcol_ref, frow_ref, o_ref,
                   qm_ref, acc_ref, s_ref, *, tq, tk):
    i = pl.program_id(1)
    nkb = (i * tq + tq - 1) // tk + 1
    q_pos = i * tq + lax.broadcasted_iota(jnp.int32, (1, tq), 1)
    _store_masked_qt(qm_ref, qt_ref, N_HEADS_C)
    f_q = frow_ref[0, 0] * LOG2E
    acc_ref[...] = jnp.zeros(acc_ref.shape, F32)

    def step(kb, ms, last):
        start = pl.multiple_of(kb * tk, tk)
        f_k = fcol_ref[0, pl.ds(start, tk), :] * LOG2E
        if last:
            causal = kb * tk + lax.broadcasted_iota(jnp.int32, (tk, 1), 0) <= q_pos

        def scores(h):
            p = h // 2
            s = _dot(k_ref[0, pl.ds(start, tk), p * LANES:(p + 1) * LANES], qm_ref[h])
            s = s + (f_q[h:h + 1, :] - f_k[:, SM_F + h:SM_F + h + 1])
            return jnp.where(causal, s, -jnp.inf) if last else s

        return _softmax_heads(N_HEADS_C, scores,
                              lambda h: vt_ref[0, kb, h * V_ROWS:(h + 1) * V_ROWS, :],
                              ms, acc_ref, s_ref)

    ms = lax.fori_loop(0, nkb - 1, lambda kb, ms: step(kb, ms, False),
                       (jnp.full((1, tq), NEG_BIG, F32),) * N_HEADS_C)
    step(nkb - 1, ms, True)
    _finish_heads(o_ref, acc_ref, N_HEADS_C)


def _forget_attention(qt, k, vt, fcol, frow, tq, tk):
    b, s, _ = k.shape
    nq, nk = s // tq, s // tk
    vrows = N_HEADS_C * V_ROWS
    return pl.pallas_call(
        functools.partial(_forget_kernel, tq=tq, tk=tk),
        grid=(b, nq),
        in_specs=[pl.BlockSpec((HW, tq), lambda bi, i: (0, bi * nq + i)),
                  pl.BlockSpec((1, s, HW), lambda bi, i: (bi, 0, 0), pipeline_mode=pl.Buffered(1)),
                  pl.BlockSpec((1, nk, vrows, tk), lambda bi, i: (bi, 0, 0, 0),
                               pipeline_mode=pl.Buffered(1)),
                  pl.BlockSpec((1, s, LANES), lambda bi, i: (bi, 0, 0), pipeline_mode=pl.Buffered(1)),
                  pl.BlockSpec((1, 1, SUBLANES, tq), lambda bi, i: (bi, i, 0, 0))],
        out_specs=pl.BlockSpec((1, tq, HW), lambda bi, i: (bi, i, 0)),
        out_shape=jax.ShapeDtypeStruct((b, s, HW), MXU_DTYPE),
        scratch_shapes=[pltpu.VMEM((N_HEADS_C, LANES, tq), MXU_DTYPE),
                        pltpu.VMEM((N_HEADS_C, V_ROWS, tq), F32),
                        pltpu.VMEM((SCORE_SLOTS, tk, tq), F32)],
        compiler_params=_cparams("arbitrary", "arbitrary"),
        name="forget_attention",
    )(qt, k, vt.reshape(b, nk, vrows, tk), fcol, frow)


def _band_kernel(qt_ref, k_ref, vt_ref, bt_ref, o_ref, qm_ref, s_ref, *, tq):
    i = pl.program_id(1)
    win = tq + BAND_PAD
    start = pl.multiple_of(i * tq, tq)
    _store_masked_qt(qm_ref, qt_ref, N_HEADS_B)
    key_pos = start - BAND_PAD + lax.broadcasted_iota(jnp.int32, (win, 1), 0)
    in_seq = key_pos >= 0
    for h in range(N_HEADS_B):
        p = h // 2
        s = _dot(k_ref[0, pl.ds(start, win), p * LANES:(p + 1) * LANES], qm_ref[h]) + bt_ref[h]
        s_ref[h] = jnp.where(in_seq, s, NEG_BIG)
    outs = []
    for h in range(N_HEADS_B):
        s = s_ref[h]
        pe = jnp.exp(s - jnp.max(s, axis=0, keepdims=True)).astype(vt_ref.dtype)
        pv = None
        for j in range(win // BAND_KEYS):
            term = _dot(vt_ref[0, i * (tq // BAND_KEYS) + j, h * V_ROWS:(h + 1) * V_ROWS, :],
                        pe[j * BAND_KEYS:(j + 1) * BAND_KEYS, :])
            pv = term if pv is None else pv + term
        outs.append(pv[:HEAD_DIM] / pv[HEAD_DIM:HEAD_DIM + 1])
    outs.append(jnp.zeros((HW - N_HEADS_B * HEAD_DIM, tq), F32))
    o_ref[0] = jnp.concatenate(outs, axis=0).T.astype(o_ref.dtype)


def _band_bias_kernel(rb_ref, o_ref, *, tq):
    h = pl.program_id(0)
    win = tq + BAND_PAD
    c = lax.broadcasted_iota(jnp.int32, (win, tq), 0)
    r = lax.broadcasted_iota(jnp.int32, (win, tq), 1)
    dist = jnp.clip(r + BAND_PAD - c, -REL_CLIP, REL_CLIP) + REL_CLIP
    back = r // CHUNK + PREV_CHUNKS - c // CHUNK
    table = lax.fori_loop(0, 2 * REL_CLIP + 1,
                          lambda j, acc: jnp.where(dist == j, rb_ref[h, j], acc),
                          jnp.zeros((win, tq), F32))
    o_ref[0] = jnp.where((back >= 0) & (back <= PREV_CHUNKS), table, NEG_BIG)


def _band_attention(qt, k, vt, rel_bias):
    b, s, _ = k.shape
    tq = min(128, s)
    win = tq + BAND_PAD
    nq = s // tq
    vrows = N_HEADS_B * V_ROWS
    pad_blocks = BAND_PAD // BAND_KEYS
    k = jnp.pad(k, ((0, 0), (BAND_PAD, 0), (0, 0)))
    vt = jnp.pad(vt.reshape(b, s // BAND_KEYS, vrows, BAND_KEYS), ((0, 0), (pad_blocks, 0), (0, 0), (0, 0)))
    bt = pl.pallas_call(
        functools.partial(_band_bias_kernel, tq=tq),
        grid=(N_HEADS_B,),
        in_specs=[pl.BlockSpec(memory_space=pltpu.SMEM)],
        out_specs=pl.BlockSpec((1, win, tq), lambda h: (h, 0, 0)),
        out_shape=jax.ShapeDtypeStruct((N_HEADS_B, win, tq), F32),
        compiler_params=_cparams("arbitrary"),
        name="band_bias",
    )(rel_bias)
    return pl.pallas_call(
        functools.partial(_band_kernel, tq=tq),
        grid=(b, nq),
        in_specs=[pl.BlockSpec((HW, tq), lambda bi, i: (0, bi * nq + i)),
                  pl.BlockSpec((1, s + BAND_PAD, HW), lambda bi, i: (bi, 0, 0)),
                  pl.BlockSpec((1, s // BAND_KEYS + pad_blocks, vrows, BAND_KEYS), lambda bi, i: (bi, 0, 0, 0)),
                  pl.BlockSpec((N_HEADS_B, win, tq), lambda bi, i: (0, 0, 0))],
        out_specs=pl.BlockSpec((1, tq, HW), lambda bi, i: (bi, i, 0)),
        out_shape=jax.ShapeDtypeStruct((b, s, HW), MXU_DTYPE),
        scratch_shapes=[pltpu.VMEM((N_HEADS_B, LANES, tq), MXU_DTYPE),
                        pltpu.VMEM((N_HEADS_B, win, tq), F32)],
        compiler_params=_cparams("arbitrary", "arbitrary"),
        name="band_attention",
    )(qt, k, vt, bt)


def _rms_mod(y, ng_ref, sc_ref, sh_ref):
    n = y * lax.rsqrt(jnp.mean(y * y, axis=-1, keepdims=True) + RMS_EPS) * ng_ref[...]
    return n if sc_ref is None else n * (1.0 + sc_ref[0]) + sh_ref[0]


def _merge_kernel(x_ref, h_ref, oa_ref, ob_ref, oc_ref, wg_ref, bg_ref, wb_ref, wo_ref, g_ref,
                  ng_ref, sc_ref, sh_ref, o_ref, hn_ref):
    h = h_ref[0]
    d = h.shape[1]
    y = None
    for br, o_br in enumerate((oa_ref, ob_ref, oc_ref)):
        gate = _sigmoid(_dot(h, wg_ref[:, br * d:(br + 1) * d]) + bg_ref[:, br * d:(br + 1) * d])
        term = gate * _dot(o_br[0], wb_ref[br])
        y = term if y is None else y + term
    x_new = x_ref[0] + g_ref[0] * _dot(y.astype(wo_ref.dtype), wo_ref[...])
    o_ref[0] = x_new
    hn_ref[0] = _rms_mod(x_new, ng_ref, sc_ref, sh_ref).astype(hn_ref.dtype)


def _merge(x, h, o_a, o_b, o_c, gates_w, gates_b, wb, w_out, mod, norm_g):
    b, s, d = x.shape
    tm = min(s, 512)
    tok = lambda n: pl.BlockSpec((1, tm, n), lambda bi, i: (bi, i, 0))
    const2 = lambda shape: pl.BlockSpec(shape, lambda bi, i: (0, 0))
    modblk = lambda j: pl.BlockSpec((1, 1, d), lambda bi, i: (bi, 0, j))
    return pl.pallas_call(
        _merge_kernel,
        grid=(b, s // tm),
        in_specs=[tok(d), tok(d), tok(HW), tok(HW), tok(HW),
                  const2((d, N_BRANCH * d)), const2((1, N_BRANCH * d)),
                  pl.BlockSpec((N_BRANCH, HW, d), lambda bi, i: (0, 0, 0)),
                  const2((d, d)), modblk(2), const2((1, d)), modblk(4), modblk(3)],
        out_specs=[tok(d), tok(d)],
        out_shape=[jax.ShapeDtypeStruct((b, s, d), F32), jax.ShapeDtypeStruct((b, s, d), MXU_DTYPE)],
        compiler_params=_cparams("arbitrary", "arbitrary"),
        name="gated_merge",
    )(x, h, o_a, o_b, o_c, gates_w, gates_b, wb, w_out, mod, norm_g.reshape(1, d), mod, mod)


def _ffn_kernel(x_ref, h_ref, wi_ref, wo_ref, g_ref, ng_ref, *rest, last):
    h = h_ref[0]
    g = _dot(h, wi_ref[:, :FFN_HIDDEN])
    u = _dot(h, wi_ref[:, FFN_HIDDEN:])
    act = (g * _sigmoid(g)) * u
    x_new = x_ref[0] + g_ref[0] * _dot(act.astype(wo_ref.dtype), wo_ref[...])
    if last:
        (o_ref,) = rest
        o_ref[0] = _rms_mod(x_new, ng_ref, None, None).astype(o_ref.dtype)
    else:
        sc_ref, sh_ref, o_ref, hn_ref = rest
        o_ref[0] = x_new
        hn_ref[0] = _rms_mod(x_new, ng_ref, sc_ref, sh_ref).astype(hn_ref.dtype)


def _ffn(x, h, w_in, w_out, mod, norm_g, next_mod):
    b, s, d = x.shape
    tm = min(s, 512)
    last = next_mod is None
    tok = pl.BlockSpec((1, tm, d), lambda bi, i: (bi, i, 0))
    resident = lambda shape: pl.BlockSpec(shape, lambda bi, i: (0, 0), pipeline_mode=pl.Buffered(1))
    modblk = lambda j: pl.BlockSpec((1, 1, d), lambda bi, i: (bi, 0, j))
    in_specs = [tok, tok, resident(w_in.shape), resident(w_out.shape), modblk(5),
                pl.BlockSpec((1, d), lambda bi, i: (0, 0))]
    args = [x, h, w_in, w_out, mod, norm_g.reshape(1, d)]
    if last:
        out_specs, out_shape = tok, jax.ShapeDtypeStruct((b, s, d), F32)
    else:
        in_specs += [modblk(1), modblk(0)]
        args += [next_mod, next_mod]
        out_specs = [tok, tok]
        out_shape = [jax.ShapeDtypeStruct((b, s, d), F32), jax.ShapeDtypeStruct((b, s, d), MXU_DTYPE)]
    return pl.pallas_call(
        functools.partial(_ffn_kernel, last=last),
        grid=(b, s // tm),
        in_specs=in_specs,
        out_specs=out_specs,
        out_shape=out_shape,
        compiler_params=_cparams("arbitrary", "arbitrary"),
        name="swiglu_ffn",
    )(*args)


def _pack_branch_weights(w_branch):
    pad = lambda a: jnp.pad(a, ((0, HW - a.shape[0]), (0, 0)))
    return jnp.stack([pad(w_branch[:WIDTH_A]),
                      pad(w_branch[WIDTH_A:WIDTH_A + WIDTH_B]),
                      pad(w_branch[WIDTH_A + WIDTH_B:])]).astype(MXU_DTYPE)


def kernel(x, c, positions, ada_w, ada_b, norm1_g, w_in, b_in, rel_bias, w_branch, w_out,
           norm2_g, w_ffn_in, w_ffn_out, final_g):
    b, s, d = x.shape
    depth = ada_w.shape[0]
    topk = min(TOPK_MAX, s // 4)
    tq, tk = min(Q_BLOCK, s), min(K_BLOCK, s)
    cos, sa, sb = _rope_tables(positions)
    mod_all = _ada_mod(c, ada_w, ada_b)
    mods = [mod_all[l].reshape(b, 1, 6 * d) for l in range(depth)]
    h = _norm(x, norm1_g[0], mods[0], 1, 0, MXU_DTYPE)
    for l in range(depth):
        mod = mods[l]
        w_all, b_all, s_all, gates_w, gates_b = _pack_in_weights(w_in, b_in, l)
        (qt_a, k_a, vt_a, qit, ki, wt, sm, qt_b, k_b, vt_b, qt_c, k_c, vt_c) = _proj(
            h.reshape(b * s, d), w_all, b_all, s_all, cos, sa, sb, tk)
        k_a, ki, sm, k_b, k_c = (a.reshape(b, s, a.shape[-1]) for a in (k_a, ki, sm, k_b, k_c))
        o_a = _dsa_attention(qt_a, k_a, vt_a, qit, ki, wt, topk, tq, tk)
        o_b = _band_attention(qt_b, k_b, vt_b, rel_bias[l])
        fcol, frow = _forget_cumsum(sm, tq)
        o_c = _forget_attention(qt_c, k_c, vt_c, fcol, frow, tq, tk)
        x, h = _merge(x, h, o_a, o_b, o_c, gates_w, gates_b, _pack_branch_weights(w_branch[l]),
                      w_out[l].astype(MXU_DTYPE), mod, norm2_g[l])
        ffn_w = (w_ffn_in[l].astype(MXU_DTYPE), w_ffn_out[l].astype(MXU_DTYPE))
        if l + 1 < depth:
            x, h = _ffn(x, h, *ffn_w, mod, norm1_g[l + 1], mods[l + 1])
        else:
            x = _ffn(x, h, *ffn_w, mod, final_g, None)
    return x
```
